```python
import jax, jax.numpy as jnp
from jax import lax
import numpy as np

D_MODEL = 1024
BATCH = 32
SEQ = 2048
DEPTH = 1
DEC_BATCH = 8
DEC_SEQ = 4096
PAST_LEN = 128

FOURIER_WIDTH = D_MODEL // 2
FOURIER_GROUPS = 4
FOURIER_GROUP_DIM = FOURIER_WIDTH // FOURIER_GROUPS
HGRN_WIDTH = D_MODEL - FOURIER_WIDTH
HGRN_HEAD_DIM = 128
HGRN_HEADS = HGRN_WIDTH // HGRN_HEAD_DIM
CHUNK = 64
IN_WIDTH = FOURIER_WIDTH + 5 * HGRN_WIDTH
N_GROUPS = 4
EXPERTS_PER_GROUP = 8
N_EXPERTS = N_GROUPS * EXPERTS_PER_GROUP
TOP_K = 2
D_EXPERT = D_MODEL // 2
PLE_DIM = 256
EPS = 1e-6

kernel_name = "fnet_hgrn2_hmoe_parallel_encoder"


def _rmsnorm(x, g):
    xf = x.astype(jnp.float32)
    y = xf * lax.rsqrt(jnp.mean(xf * xf, axis=-1, keepdims=True) + EPS) * g.astype(jnp.float32)
    return y.astype(x.dtype)


def _fourier_mix(u, w):
    B, S, _ = u.shape
    u4 = u.astype(jnp.float32).reshape(B, S, FOURIER_GROUPS, FOURIER_GROUP_DIM)
    y = jnp.fft.fft2(u4, axes=(1, 3), norm="ortho").real
    y = jnp.einsum('bsgc,gcd->bsgd', y, w.astype(jnp.float32))
    return y.reshape(B, S, FOURIER_WIDTH).astype(u.dtype)


def _hgrn2_direction(q, k, v, log_f):
    B, S, H, D = q.shape
    N = S // CHUNK

    def to_chunks(t):
        return t.astype(jnp.float32).reshape(B, N, CHUNK, H, D).transpose(0, 3, 1, 2, 4)

    q, k, v, log_f = (to_chunks(t) for t in (q, k, v, log_f))
    b = jnp.cumsum(log_f, axis=3)
    b_last = b[:, :, :, -1:, :]
    q_dec = q * jnp.exp(b) * (D ** -0.5)
    scores = jnp.einsum('bhncd,bhnsd->bhncs', q_dec, k * jnp.exp(-b))
    lower = jnp.tril(jnp.ones((CHUNK, CHUNK), dtype=bool))
    scores = jnp.where(lower, scores, 0.0)
    o_intra = jnp.einsum('bhncs,bhnse->bhnce', scores, v)
    kv = jnp.einsum('bhncd,bhnce->bhnde', k * jnp.exp(b_last - b), v)
    decay = jnp.exp(b_last[:, :, :, 0, :])

    def step(state, xs):
        q_n, dec_n, kv_n = xs
        o_n = jnp.einsum('bhcd,bhde->bhce', q_n, state)
        return dec_n[..., None] * state + kv_n, o_n

    state0 = jnp.zeros((B, H, D, D), jnp.float32)
    _, o_inter = lax.scan(step, state0,
                          (jnp.moveaxis(q_dec, 2, 0), jnp.moveaxis(decay, 2, 0), jnp.moveaxis(kv, 2, 0)))
    o = o_intra + jnp.moveaxis(o_inter, 0, 2)
    return o.transpose(0, 2, 3, 1, 4).reshape(B, S, H, D)


def _gate_terms(f_raw, lb):
    fr = f_raw.astype(jnp.float32)
    f = lb + (1.0 - lb) * jax.nn.sigmoid(fr)
    k = (1.0 - lb) * jax.nn.sigmoid(-fr)
    return k, jnp.log(f)


def _hgrn2_bidirectional(q, v, f_raw_fwd, f_raw_bwd, lb_fwd, lb_bwd):
    B, S, _ = q.shape
    shp = (B, S, HGRN_HEADS, HGRN_HEAD_DIM)
    q = q.reshape(shp)
    v = v.reshape(shp)
    k_f, lf_f = _gate_terms(f_raw_fwd.reshape(shp), lb_fwd.reshape(HGRN_HEADS, HGRN_HEAD_DIM))
    k_b, lf_b = _gate_terms(f_raw_bwd.reshape(shp), lb_bwd.reshape(HGRN_HEADS, HGRN_HEAD_DIM))
    o_fwd = _hgrn2_direction(q, k_f, v, lf_f)
    o_bwd = _hgrn2_direction(q[:, ::-1], k_b[:, ::-1], v[:, ::-1], lf_b[:, ::-1])[:, ::-1]
    return o_fwd + o_bwd


def _hier_moe(h, w_rg, w_re, w_eg, w_eu, w_ed):
    B, S, D = h.shape
    t = h.reshape(-1, D)
    T = t.shape[0]
    p_group = jax.nn.softmax((t @ w_rg).astype(jnp.float32), axis=-1)
    p_top, g_idx = lax.top_k(p_group, 1)
    le = (t @ w_re).astype(jnp.float32).reshape(T, N_GROUPS, EXPERTS_PER_GROUP)
    le_sel = jnp.take_along_axis(le, g_idx[:, :, None], axis=1)[:, 0]
    v2, i2 = lax.top_k(le_sel, TOP_K)
    w2 = jax.nn.softmax(v2, axis=-1) * p_top
    ids = g_idx * EXPERTS_PER_GROUP + i2
    gates = jnp.sum(jax.nn.one_hot(ids, N_EXPERTS, dtype=jnp.float32) * w2[..., None], axis=1)
    y = jnp.zeros((T, D), jnp.float32)
    for e in range(N_EXPERTS):
        hid = jax.nn.silu(t @ w_eg[e]) * (t @ w_eu[e])
        y = y + gates[:, e:e + 1] * (hid @ w_ed[e]).astype(jnp.float32)
    return y.astype(h.dtype).reshape(B, S, D)


def _trunk(x, p, norm_mix, w_in, w_fourier, lb_logits, norm_o, w_out, norm_ffn,
           w_route_group, w_route_expert, w_exp_gate, w_exp_up, w_exp_down,
           norm_ple, w_ple_gate, w_ple_proj, norm_final):
    lb_all = jnp.cumsum(jax.nn.softmax(lb_logits.astype(jnp.float32), axis=0), axis=0)
    splits = [FOURIER_WIDTH + j * HGRN_WIDTH for j in range(5)]
    for i in range(DEPTH):
        B, S, _ = x.shape
        h = _rmsnorm(x, norm_mix[i])
        z = h @ w_in[i]
        u, q, v, fr_f, fr_b, og = jnp.split(z, splits, axis=-1)
        y_four = _fourier_mix(u, w_fourier[i])
        o = _hgrn2_bidirectional(q, v, fr_f, fr_b, lb_all[i, 0], lb_all[i, 1])
        o = _rmsnorm(o, norm_o[i]) * jax.nn.silu(og.astype(jnp.float32).reshape(B, S, HGRN_HEADS, HGRN_HEAD_DIM))
        o = o.reshape(B, S, HGRN_WIDTH).astype(x.dtype)
        x = x + jnp.concatenate([y_four, o], axis=-1) @ w_out[i]
        x = x + _hier_moe(_rmsnorm(x, norm_ffn[i]), w_route_group[i], w_route_expert[i],
                          w_exp_gate[i], w_exp_up[i], w_exp_down[i])
        gate = jax.nn.sigmoid(_rmsnorm(x, norm_ple[i]) @ w_ple_gate[i])
        x = x + (p[i] @ w_ple_proj[i]) * gate
    return _rmsnorm(x, norm_final)


def setup_inputs(seed: int = 0) -> dict:
    key = jax.random.key(seed)
    ks = jax.random.split(key, 20)
    f32 = jnp.float32

    def nrm(k, shape, fan_in):
        return jax.random.normal(k, shape, f32) * (fan_in ** -0.5)

    def gain(k, shape):
        return 1.0 + 0.05 * jax.random.normal(k, shape, f32)

    return {
        "x_prompt": jax.random.normal(ks[0], (BATCH, SEQ, D_MODEL), f32),
        "x_sample": jax.random.normal(ks[1], (DEC_BATCH, DEC_SEQ, D_MODEL), f32),
        "p_prompt": jax.random.normal(ks[2], (DEPTH, BATCH, SEQ, PLE_DIM), f32),
        "p_sample": jax.random.normal(ks[3], (DEPTH, DEC_BATCH, DEC_SEQ, PLE_DIM), f32),
        "norm_mix": gain(ks[4], (DEPTH, D_MODEL)),
        "w_in": nrm(ks[5], (DEPTH, D_MODEL, IN_WIDTH), D_MODEL),
        "w_fourier": nrm(ks[6], (DEPTH, FOURIER_GROUPS, FOURIER_GROUP_DIM, FOURIER_GROUP_DIM), FOURIER_GROUP_DIM),
        "lb_logits": 0.1 * jax.random.normal(ks[7], (DEPTH + 1, 2, HGRN_WIDTH), f32),
        "norm_o": gain(ks[8], (DEPTH, HGRN_HEAD_DIM)),
        "w_out": nrm(ks[9], (DEPTH, D_MODEL, D_MODEL), D_MODEL),
        "norm_ffn": gain(ks[10], (DEPTH, D_MODEL)),
        "w_route_group": nrm(ks[11], (DEPTH, D_MODEL, N_GROUPS), D_MODEL),
        "w_route_expert": nrm(ks[12], (DEPTH, D_MODEL, N_EXPERTS), D_MODEL),
        "w_exp_gate": nrm(ks[13], (DEPTH, N_EXPERTS, D_MODEL, D_EXPERT), D_MODEL),
        "w_exp_up": nrm(ks[14], (DEPTH, N_EXPERTS, D_MODEL, D_EXPERT), D_MODEL),
        "w_exp_down": nrm(ks[15], (DEPTH, N_EXPERTS, D_EXPERT, D_MODEL), D_EXPERT),
        "norm_ple": gain(ks[16], (DEPTH, D_MODEL)),
        "w_ple_gate": nrm(ks[17], (DEPTH, D_MODEL, D_MODEL), D_MODEL),
        "w_ple_proj": nrm(ks[18], (DEPTH, PLE_DIM, D_MODEL), PLE_DIM),
        "norm_final": gain(ks[19], (D_MODEL,)),
    }


def reference(x_prompt, x_sample, p_prompt, p_sample, norm_mix, w_in, w_fourier, lb_logits, norm_o,
              w_out, norm_ffn, w_route_group, w_route_expert, w_exp_gate, w_exp_up, w_exp_down,
              norm_ple, w_ple_gate, w_ple_proj, norm_final):
    y_prompt = _trunk(x_prompt, p_prompt, norm_mix, w_in, w_fourier, lb_logits, norm_o, w_out, norm_ffn,
                      w_route_group, w_route_expert, w_exp_gate, w_exp_up, w_exp_down,
                      norm_ple, w_ple_gate, w_ple_proj, norm_final)
    y_sample = _trunk(x_sample, p_sample, norm_mix, w_in, w_fourier, lb_logits, norm_o, w_out, norm_ffn,
                      w_route_group, w_route_expert, w_exp_gate, w_exp_up, w_exp_down,
                      norm_ple, w_ple_gate, w_ple_proj, norm_final)
    return (y_prompt, y_sample)
```

```python
import functools
import math

import jax
import jax.numpy as jnp
from jax import lax
from jax.experimental import pallas as pl
from jax.experimental.pallas import tpu as pltpu

F32 = jnp.float32
BF16 = jnp.bfloat16

D_MODEL = 1024
FOURIER_WIDTH = 512
FOURIER_GROUPS = 4
GROUP_DIM = 128
HGRN_WIDTH = 512
HEAD_DIM = 128
HGRN_HEADS = 4
CHUNK = 64
IN_WIDTH = FOURIER_WIDTH + 5 * HGRN_WIDTH
N_GROUPS = 4
EXPERTS_PER_GROUP = 8
N_EXPERTS = 32
D_EXPERT = 512
PLE_DIM = 256
EPS = 1e-6
LANES = 128
VMEM_LIMIT = 56 * 1024 * 1024


def _params(*sem):
    return pltpu.CompilerParams(dimension_semantics=sem, vmem_limit_bytes=VMEM_LIMIT)


def _dot(a, b):
    return jnp.dot(a, b, preferred_element_type=F32)


def _dot_nt(a, b):
    return lax.dot_general(a, b, (((1,), (1,)), ((), ())), preferred_element_type=F32)


def _dot_tn(a, b):
    return lax.dot_general(a, b, (((0,), (0,)), ((), ())), preferred_element_type=F32)


def _rms(x, g):
    return x * lax.rsqrt(jnp.mean(x * x, axis=-1, keepdims=True) + EPS) * g


def _inproj_kernel(x_ref, g_ref, w_ref, z_ref, h_ref):
    @pl.when(pl.program_id(1) == 0)
    def _():
        h_ref[...] = _rms(x_ref[...], g_ref[...]).astype(BF16)

    z_ref[...] = _dot(h_ref[...], w_ref[...]).astype(z_ref.dtype)


def _inproj(x2d, g, w_bf, tile):
    T = x2d.shape[0]
    tn = 1024
    return pl.pallas_call(
        _inproj_kernel,
        out_shape=jax.ShapeDtypeStruct((T, IN_WIDTH), BF16),
        grid=(T // tile, IN_WIDTH // tn),
        in_specs=[
            pl.BlockSpec((tile, D_MODEL), lambda i, j: (i, 0)),
            pl.BlockSpec((1, D_MODEL), lambda i, j: (0, 0)),
            pl.BlockSpec((D_MODEL, tn), lambda i, j: (0, j)),
        ],
        out_specs=pl.BlockSpec((tile, tn), lambda i, j: (i, j)),
        scratch_shapes=[pltpu.VMEM((tile, D_MODEL), BF16)],
        compiler_params=_params("parallel", "arbitrary"),
    )(x2d, g, w_bf)


def _fourier_fold_kernel(cc_ref, sc_ref, w_ref, m_ref, *, scale):
    for g in range(FOURIER_GROUPS):
        w = w_ref[g]
        m_ref[g, :GROUP_DIM, :] = (jnp.dot(cc_ref[...], w, preferred_element_type=F32,
                                           precision=lax.Precision.HIGHEST) * scale).astype(m_ref.dtype)
        m_ref[g, GROUP_DIM:, :] = (jnp.dot(sc_ref[...], w, preferred_element_type=F32,
                                           precision=lax.Precision.HIGHEST) * (-scale)).astype(m_ref.dtype)


def _fourier_fold(w_fourier, seq):
    cc, sc = _dft_tables(GROUP_DIM, F32)
    scale = 1.0 / math.sqrt(seq * GROUP_DIM)
    return pl.pallas_call(
        functools.partial(_fourier_fold_kernel, scale=scale),
        out_shape=jax.ShapeDtypeStruct((FOURIER_GROUPS, 2 * GROUP_DIM, GROUP_DIM), BF16),
    )(cc, sc, w_fourier)


def _dft_tables(n, dtype):
    r = int(round(math.sqrt(n)))
    while n % r:
        r -= 1
    m = jnp.arange(n, dtype=jnp.int32)[None, :]
    k_lo = jnp.arange(r, dtype=jnp.int32)[:, None]
    k_hi = jnp.arange(n // r, dtype=jnp.int32)[:, None] * r
    step = 2.0 * math.pi / n
    a_lo = ((k_lo * m) % n).astype(F32) * step
    a_hi = ((k_hi * m) % n).astype(F32) * step
    c_lo, s_lo = jnp.cos(a_lo)[None], jnp.sin(a_lo)[None]
    c_hi, s_hi = jnp.cos(a_hi)[:, None], jnp.sin(a_hi)[:, None]
    cos = (c_hi * c_lo - s_hi * s_lo).reshape(n, n)
    sin = (s_hi * c_lo + c_hi * s_lo).reshape(n, n)
    return cos.astype(dtype), sin.astype(dtype)


def _fourier_kernel(u_ref, cos_ref, sin_ref, m_ref, y_ref):
    u = u_ref[0]
    a = _dot(cos_ref[...], u).astype(BF16)
    b = _dot(sin_ref[...], u).astype(BF16)
    for g in range(FOURIER_GROUPS):
        sl = slice(g * GROUP_DIM, (g + 1) * GROUP_DIM)
        ab = jnp.concatenate([a[:, sl], b[:, sl]], axis=1)
        y_ref[0, :, sl] = _dot(ab, m_ref[g]).astype(y_ref.dtype)


def _fourier(z3, w_fourier, tk=512):
    B, S, _ = z3.shape
    cos, sin = _dft_tables(S, BF16)
    m = _fourier_fold(w_fourier, S)
    return pl.pallas_call(
        _fourier_kernel,
        out_shape=jax.ShapeDtypeStruct((B, S, FOURIER_WIDTH), BF16),
        grid=(B, S // tk),
        in_specs=[
            pl.BlockSpec((1, S, FOURIER_WIDTH), lambda b, k: (b, 0, 0)),
            pl.BlockSpec((tk, S), lambda b, k: (k, 0)),
            pl.BlockSpec((tk, S), lambda b, k: (k, 0)),
            pl.BlockSpec((FOURIER_GROUPS, 2 * GROUP_DIM, GROUP_DIM), lambda b, k: (0, 0, 0)),
        ],
        out_specs=pl.BlockSpec((1, tk, FOURIER_WIDTH), lambda b, k: (b, k, 0)),
        compiler_params=_params("parallel", "arbitrary"),
    )(z3, cos, sin, m)


def _split3(x):
    hi = x.astype(BF16)
    r1 = x - hi.astype(F32)
    mid = r1.astype(BF16)
    lo = (r1 - mid.astype(F32)).astype(BF16)
    return hi, mid, lo


def _hgrn_kernel(q_ref, v_ref, ff_ref, fb_ref, og_ref, lb_ref, go_ref, o_ref, acc_ref, *, seq):
    n_chunks = seq // CHUNK
    row = lax.broadcasted_iota(jnp.int32, (CHUNK, CHUNK), 0)
    col = lax.broadcasted_iota(jnp.int32, (CHUNK, CHUNK), 1)
    lower = row >= col
    upper = row <= col
    tri_fwd = jnp.where(lower, 1.0, 0.0).astype(BF16)
    tri_bwd = jnp.where(upper, 1.0, 0.0).astype(BF16)
    last_fwd = (row == CHUNK - 1)
    inv_sqrt_d = HEAD_DIM ** -0.5

    def chunk_out(n, state_t, f_ref, lb, tri, mask, edge_row):
        rows = pl.ds(pl.multiple_of(n * CHUNK, CHUNK), CHUNK)
        q = q_ref[0, rows, :].astype(F32)
        v = v_ref[0, rows, :]
        fr = f_ref[0, rows, :].astype(F32)
        f = lb + (1.0 - lb) * jax.nn.sigmoid(fr)
        k = (1.0 - lb) * jax.nn.sigmoid(-fr)
        hi, mid, lo = _split3(jnp.log(f))
        b = _dot(tri, hi) + _dot(tri, mid) + _dot(tri, lo)
        b_edge = b[edge_row:edge_row + 1, :]
        q_dec = (q * jnp.exp(b) * inv_sqrt_d).astype(BF16)
        k_inv = (k * jnp.exp(-b)).astype(BF16)
        k_dec = (k * jnp.exp(b_edge - b)).astype(BF16)
        scores = jnp.where(mask, _dot_nt(q_dec, k_inv), 0.0).astype(BF16)
        o = _dot(scores, v) + _dot_nt(q_dec, state_t.astype(BF16))
        new_state_t = state_t * jnp.exp(b_edge) + _dot_tn(v, k_dec)
        return rows, o, new_state_t

    state0 = jnp.zeros((HEAD_DIM, HEAD_DIM), F32)

    def fwd_body(i, state_t):
        rows, o, new_state_t = chunk_out(i, state_t, ff_ref, lb_ref[0:1, :], tri_fwd, lower, CHUNK - 1)
        acc_ref[rows, :] = o
        return new_state_t

    lax.fori_loop(0, n_chunks, fwd_body, state0)

    def bwd_body(i, state_t):
        n = n_chunks - 1 - i
        rows, o, new_state_t = chunk_out(n, state_t, fb_ref, lb_ref[1:2, :], tri_bwd, upper, 0)
        o = o + acc_ref[rows, :]
        og = og_ref[0, rows, :].astype(F32)
        o_ref[0, rows, :] = (_rms(o, go_ref[...]) * (og * jax.nn.sigmoid(og))).astype(o_ref.dtype)
        return new_state_t

    lax.fori_loop(0, n_chunks, bwd_body, state0)


def _hgrn(z3, lb, g_o):
    B, S, _ = z3.shape
    nblk = HGRN_WIDTH // HEAD_DIM
    base = FOURIER_WIDTH // HEAD_DIM

    def zspec(j):
        return pl.BlockSpec((1, S, HEAD_DIM), lambda b, h, j=j: (b, 0, base + j * nblk + h))

    return pl.pallas_call(
        functools.partial(_hgrn_kernel, seq=S),
        out_shape=jax.ShapeDtypeStruct((B, S, HGRN_WIDTH), BF16),
        grid=(B, HGRN_HEADS),
        in_specs=[zspec(0), zspec(1), zspec(2), zspec(3), zspec(4),
                  pl.BlockSpec((2, HEAD_DIM), lambda b, h: (0, h)),
                  pl.BlockSpec((1, HEAD_DIM), lambda b, h: (0, 0))],
        out_specs=pl.BlockSpec((1, S, HEAD_DIM), lambda b, h: (b, 0, h)),
        scratch_shapes=[pltpu.VMEM((S, HEAD_DIM), F32)],
        compiler_params=_params("parallel", "arbitrary"),
    )(z3, z3, z3, z3, z3, lb, g_o)


def _outproj_kernel(x_ref, yf_ref, o_ref, wa_ref, wb_ref, g_ref, wr_ref, x1_ref, h2_ref, gates_ref):
    x1 = x_ref[...] + _dot(yf_ref[...], wa_ref[...]) + _dot(o_ref[...], wb_ref[...])
    x1_ref[...] = x1
    h2 = _rms(x1, g_ref[...]).astype(BF16)
    h2_ref[...] = h2
    logits = _dot(h2, wr_ref[...])
    lane = lax.broadcasted_iota(jnp.int32, logits.shape, 1)
    neg = -jnp.inf

    def argmax_lowest(x, m):
        return jnp.min(jnp.where(x == m, lane, LANES), axis=1, keepdims=True)

    lg = jnp.where((lane >= N_EXPERTS) & (lane < N_EXPERTS + N_GROUPS), logits, neg)
    mg = jnp.max(lg, axis=1, keepdims=True)
    p_top = 1.0 / jnp.sum(jnp.exp(lg - mg), axis=1, keepdims=True)
    g_idx = argmax_lowest(lg, mg) - N_EXPERTS
    lo = g_idx * EXPERTS_PER_GROUP
    le = jnp.where((lane >= lo) & (lane < lo + EXPERTS_PER_GROUP), logits, neg)
    v1 = jnp.max(le, axis=1, keepdims=True)
    i1 = argmax_lowest(le, v1)
    le2 = jnp.where(lane == i1, neg, le)
    v2 = jnp.max(le2, axis=1, keepdims=True)
    i2 = argmax_lowest(le2, v2)
    e2 = jnp.exp(v2 - v1)
    w1 = p_top / (1.0 + e2)
    w2 = w1 * e2
    gates_ref[...] = jnp.where(lane == i1, w1, 0.0) + jnp.where(lane == i2, w2, 0.0)


def _outproj(x2d, yf2d, o2d, wa, wb, g, wr, tile):
    T = x2d.shape[0]
    tok = lambda w: pl.BlockSpec((tile, w), lambda i: (i, 0))
    full = lambda a: pl.BlockSpec(a.shape, lambda i: (0,) * a.ndim)
    return pl.pallas_call(
        _outproj_kernel,
        out_shape=(jax.ShapeDtypeStruct((T, D_MODEL), F32),
                   jax.ShapeDtypeStruct((T, D_MODEL), BF16),
                   jax.ShapeDtypeStruct((T, LANES), F32)),
        grid=(T // tile,),
        in_specs=[tok(D_MODEL), tok(FOURIER_WIDTH), tok(HGRN_WIDTH), full(wa), full(wb), full(g), full(wr)],
        out_specs=(tok(D_MODEL), tok(D_MODEL), tok(LANES)),
        compiler_params=_params("parallel"),
    )(x2d, yf2d, o2d, wa, wb, g, wr)


def _moe_kernel(h_ref, gates_ref, wg_ref, wu_ref, wd_ref, y_ref):
    e = pl.program_id(1)

    @pl.when(e == 0)
    def _():
        y_ref[...] = jnp.zeros_like(y_ref)

    h = h_ref[...]
    lane = lax.broadcasted_iota(jnp.int32, gates_ref.shape, 1)
    gate = jnp.sum(jnp.where(lane == e, gates_ref[...], 0.0), axis=1, keepdims=True)
    a = _dot(h, wg_ref[0])
    hid = (a * jax.nn.sigmoid(a) * _dot(h, wu_ref[0])).astype(BF16)
    y_ref[...] += gate * _dot(hid, wd_ref[0])


def _moe(h2, gates, wg, wu, wd, tile):
    T = h2.shape[0]
    return pl.pallas_call(
        _moe_kernel,
        out_shape=jax.ShapeDtypeStruct((T, D_MODEL), F32),
        grid=(T // tile, N_EXPERTS),
        in_specs=[
            pl.BlockSpec((tile, D_MODEL), lambda i, e: (i, 0)),
            pl.BlockSpec((tile, LANES), lambda i, e: (i, 0)),
            pl.BlockSpec((1, D_MODEL, D_EXPERT), lambda i, e: (e, 0, 0)),
            pl.BlockSpec((1, D_MODEL, D_EXPERT), lambda i, e: (e, 0, 0)),
            pl.BlockSpec((1, D_EXPERT, D_MODEL), lambda i, e: (e, 0, 0)),
        ],
        out_specs=pl.BlockSpec((tile, D_MODEL), lambda i, e: (i, 0)),
        compiler_params=_params("parallel", "arbitrary"),
    )(h2, gates, wg, wu, wd)


def _tail_kernel(x1_ref, y_ref, p_ref, gp_ref, wpg_ref, wpp_ref, gf_ref, out_ref):
    x2 = x1_ref[...] + y_ref[...]
    hp = _rms(x2, gp_ref[...]).astype(BF16)
    gate = jax.nn.sigmoid(_dot(hp, wpg_ref[...]))
    x3 = x2 + _dot(p_ref[...].astype(BF16), wpp_ref[...]) * gate
    out_ref[...] = _rms(x3, gf_ref[...])


def _tail(x1, y, p2d, gp, wpg, wpp, gf, tile):
    T = x1.shape[0]
    tok = lambda w: pl.BlockSpec((tile, w), lambda i: (i, 0))
    full = lambda a: pl.BlockSpec(a.shape, lambda i: (0,) * a.ndim)
    return pl.pallas_call(
        _tail_kernel,
        out_shape=jax.ShapeDtypeStruct((T, D_MODEL), F32),
        grid=(T // tile,),
        in_specs=[tok(D_MODEL), tok(D_MODEL), tok(PLE_DIM), full(gp), full(wpg), full(wpp), full(gf)],
        out_specs=tok(D_MODEL),
        compiler_params=_params("parallel"),
    )(x1, y, p2d, gp, wpg, wpp, gf)


def _trunk(x, p, w):
    B, S, _ = x.shape
    T = B * S
    x2d = x.reshape(T, D_MODEL)
    z = _inproj(x2d, w["g_mix"], w["w_in"], tile=512)
    z3 = z.reshape(B, S, IN_WIDTH)
    yf = _fourier(z3, w["w_fourier"])
    o = _hgrn(z3, w["lb"], w["g_o"])
    x1, h2, gates = _outproj(x2d, yf.reshape(T, FOURIER_WIDTH), o.reshape(T, HGRN_WIDTH),
                             w["w_out_a"], w["w_out_b"], w["g_ffn"], w["w_router"], tile=512)
    y = _moe(h2, gates, w["w_eg"], w["w_eu"], w["w_ed"], tile=1024)
    out = _tail(x1, y, p.reshape(T, PLE_DIM), w["g_ple"], w["w_ple_gate"], w["w_ple_proj"],
                w["g_final"], tile=512)
    return out.reshape(B, S, D_MODEL)


def kernel(x_prompt, x_sample, p_prompt, p_sample, norm_mix, w_in, w_fourier, lb_logits, norm_o, w_out,
           norm_ffn, w_route_group, w_route_expert, w_exp_gate, w_exp_up, w_exp_down, norm_ple,
           w_ple_gate, w_ple_proj, norm_final):
    assert w_in.shape[0] == 1, "single-layer trunk"
    lb_all = jnp.cumsum(jax.nn.softmax(lb_logits.astype(F32), axis=0), axis=0)
    router = jnp.concatenate(
        [w_route_expert[0], w_route_group[0],
         jnp.zeros((D_MODEL, LANES - N_EXPERTS - N_GROUPS), F32)], axis=1)
    w = {
        "g_mix": norm_mix[0][None, :],
        "w_in": w_in[0].astype(BF16),
        "w_fourier": w_fourier[0],
        "lb": lb_all[0],
        "g_o": norm_o[0][None, :],
        "w_out_a": w_out[0, :FOURIER_WIDTH].astype(BF16),
        "w_out_b": w_out[0, FOURIER_WIDTH:].astype(BF16),
        "g_ffn": norm_ffn[0][None, :],
        "w_router": router.astype(BF16),
        "w_eg": w_exp_gate[0].astype(BF16),
        "w_eu": w_exp_up[0].astype(BF16),
        "w_ed": w_exp_down[0].astype(BF16),
        "g_ple": norm_ple[0][None, :],
        "w_ple_gate": w_ple_gate[0].astype(BF16),
        "w_ple_proj": w_ple_proj[0].astype(BF16),
        "g_final": norm_final[None, :],
    }
    return (_trunk(x_prompt, p_prompt[0], w), _trunk(x_sample, p_sample[0], w))
```

```python
import functools
import math

import jax
import jax.numpy as jnp
from jax import lax
from jax.experimental import pallas as pl
from jax.experimental.pallas import tpu as pltpu

F32 = jnp.float32
BF16 = jnp.bfloat16

D_MODEL = 1024
FOURIER_WIDTH = 512
FOURIER_GROUPS = 4
GROUP_DIM = 128
HGRN_WIDTH = 512
HEAD_DIM = 128
HGRN_HEADS = 4
CHUNK = 64
IN_WIDTH = FOURIER_WIDTH + 5 * HGRN_WIDTH
N_GROUPS = 4
EXPERTS_PER_GROUP = 8
N_EXPERTS = 32
D_EXPERT = 512
PLE_DIM = 256
EPS = 1e-6
LANES = 128
SUBLANES = 8
VMEM_LIMIT = 56 * 1024 * 1024
ROUTE_TILE = 512
EXPERT_TILE = 512
ROW = (SUBLANES, LANES)


def _params(*sem):
    return pltpu.CompilerParams(dimension_semantics=sem, vmem_limit_bytes=VMEM_LIMIT)


def _dot(a, b):
    return jnp.dot(a, b, preferred_element_type=F32)


def _dot_nt(a, b):
    return lax.dot_general(a, b, (((1,), (1,)), ((), ())), preferred_element_type=F32)


def _dot_tn(a, b):
    return lax.dot_general(a, b, (((0,), (0,)), ((), ())), preferred_element_type=F32)


def _rms(x, g):
    return x * lax.rsqrt(jnp.mean(x * x, axis=-1, keepdims=True) + EPS) * g


def _inproj_kernel(x_ref, g_ref, w_ref, z_ref, h_ref):
    @pl.when(pl.program_id(1) == 0)
    def _():
        h_ref[...] = _rms(x_ref[...], g_ref[...]).astype(BF16)

    z_ref[...] = _dot(h_ref[...], w_ref[...]).astype(z_ref.dtype)


def _inproj(x2d, g, w_bf, tile):
    T = x2d.shape[0]
    tn = 1024
    return pl.pallas_call(
        _inproj_kernel,
        out_shape=jax.ShapeDtypeStruct((T, IN_WIDTH), BF16),
        grid=(T // tile, IN_WIDTH // tn),
        in_specs=[
            pl.BlockSpec((tile, D_MODEL), lambda i, j: (i, 0)),
            pl.BlockSpec((1, D_MODEL), lambda i, j: (0, 0)),
            pl.BlockSpec((D_MODEL, tn), lambda i, j: (0, j)),
        ],
        out_specs=pl.BlockSpec((tile, tn), lambda i, j: (i, j)),
        scratch_shapes=[pltpu.VMEM((tile, D_MODEL), BF16)],
        compiler_params=_params("parallel", "arbitrary"),
        name="inproj",
    )(x2d, g, w_bf)


def _fourier_fold_kernel(cc_ref, sc_ref, w_ref, m_ref, *, scale):
    for g in range(FOURIER_GROUPS):
        w = w_ref[g]
        m_ref[g, :GROUP_DIM, :] = (jnp.dot(cc_ref[...], w, preferred_element_type=F32,
                                           precision=lax.Precision.HIGHEST) * scale).astype(m_ref.dtype)
        m_ref[g, GROUP_DIM:, :] = (jnp.dot(sc_ref[...], w, preferred_element_type=F32,
                                           precision=lax.Precision.HIGHEST) * (-scale)).astype(m_ref.dtype)


def _fourier_fold(w_fourier, seq):
    cc, sc = _dft_tables(GROUP_DIM, F32)
    scale = 1.0 / math.sqrt(seq * GROUP_DIM)
    return pl.pallas_call(
        functools.partial(_fourier_fold_kernel, scale=scale),
        out_shape=jax.ShapeDtypeStruct((FOURIER_GROUPS, 2 * GROUP_DIM, GROUP_DIM), BF16),
        name="fourier_fold",
    )(cc, sc, w_fourier)


def _dft_tables(n, dtype):
    r = int(round(math.sqrt(n)))
    while n % r:
        r -= 1
    m = jnp.arange(n, dtype=jnp.int32)[None, :]
    k_lo = jnp.arange(r, dtype=jnp.int32)[:, None]
    k_hi = jnp.arange(n // r, dtype=jnp.int32)[:, None] * r
    step = 2.0 * math.pi / n
    a_lo = ((k_lo * m) % n).astype(F32) * step
    a_hi = ((k_hi * m) % n).astype(F32) * step
    c_lo, s_lo = jnp.cos(a_lo)[None], jnp.sin(a_lo)[None]
    c_hi, s_hi = jnp.cos(a_hi)[:, None], jnp.sin(a_hi)[:, None]
    cos = (c_hi * c_lo - s_hi * s_lo).reshape(n, n)
    sin = (s_hi * c_lo + c_hi * s_lo).reshape(n, n)
    return cos.astype(dtype), sin.astype(dtype)


def _fourier_kernel(u_ref, cos_ref, sin_ref, m_ref, y_ref):
    u = u_ref[0]
    a = _dot(cos_ref[...], u).astype(BF16)
    b = _dot(sin_ref[...], u).astype(BF16)
    for g in range(FOURIER_GROUPS):
        sl = slice(g * GROUP_DIM, (g + 1) * GROUP_DIM)
        ab = jnp.concatenate([a[:, sl], b[:, sl]], axis=1)
        y_ref[0, :, sl] = _dot(ab, m_ref[g]).astype(y_ref.dtype)


def _fourier(z3, w_fourier, tk=512):
    B, S, _ = z3.shape
    cos, sin = _dft_tables(S, BF16)
    m = _fourier_fold(w_fourier, S)
    return pl.pallas_call(
        _fourier_kernel,
        out_shape=jax.ShapeDtypeStruct((B, S, FOURIER_WIDTH), BF16),
        grid=(B, S // tk),
        in_specs=[
            pl.BlockSpec((1, S, FOURIER_WIDTH), lambda b, k: (b, 0, 0)),
            pl.BlockSpec((tk, S), lambda b, k: (k, 0)),
            pl.BlockSpec((tk, S), lambda b, k: (k, 0)),
            pl.BlockSpec((FOURIER_GROUPS, 2 * GROUP_DIM, GROUP_DIM), lambda b, k: (0, 0, 0)),
        ],
        out_specs=pl.BlockSpec((1, tk, FOURIER_WIDTH), lambda b, k: (b, k, 0)),
        compiler_params=_params("parallel", "arbitrary"),
        name="fourier",
    )(z3, cos, sin, m)


def _split3(x):
    hi = x.astype(BF16)
    r1 = x - hi.astype(F32)
    mid = r1.astype(BF16)
    lo = (r1 - mid.astype(F32)).astype(BF16)
    return hi, mid, lo


def _hgrn_kernel(q_ref, v_ref, ff_ref, fb_ref, og_ref, lb_ref, go_ref, o_ref, acc_ref, st_ref, *, seq, heads):
    n_chunks = seq // CHUNK
    half = n_chunks // 2
    row = lax.broadcasted_iota(jnp.int32, (CHUNK, CHUNK), 0)
    col = lax.broadcasted_iota(jnp.int32, (CHUNK, CHUNK), 1)
    masks = (row >= col, row <= col)
    tris = tuple(jnp.where(m, 1.0, 0.0).astype(BF16) for m in masks)
    edges = (CHUNK - 1, 0)
    f_refs = (ff_ref, fb_ref)
    inv_sqrt_d = HEAD_DIM ** -0.5

    st_ref[...] = jnp.zeros_like(st_ref)

    def direction(c, d):
        rows = pl.ds(pl.multiple_of(c * CHUNK, CHUNK), CHUNK)
        q = q_ref[0, rows, :].astype(F32)
        v = v_ref[0, rows, :]
        fr = f_refs[d][0, rows, :].astype(F32)
        lb = lb_ref[d:d + 1, :]
        f = lb + (1.0 - lb) * jax.nn.sigmoid(fr)
        k = (1.0 - lb) * jax.nn.sigmoid(-fr)
        hi, mid, lo = _split3(jnp.log(f))
        b = _dot(tris[d], hi) + _dot(tris[d], mid) + _dot(tris[d], lo)
        b_edge = b[edges[d]:edges[d] + 1, :]
        q_dec = (q * jnp.exp(b) * inv_sqrt_d).astype(BF16)
        k_inv = (k * jnp.exp(-b)).astype(BF16)
        k_dec = (k * jnp.exp(b_edge - b)).astype(BF16)
        dec = jnp.exp(b_edge)
        outs = []
        for h in range(heads):
            sl = slice(h * HEAD_DIM, (h + 1) * HEAD_DIM)
            st = st_ref[d, h]
            scores = jnp.where(masks[d], _dot_nt(q_dec[:, sl], k_inv[:, sl]), 0.0).astype(BF16)
            outs.append(_dot(scores, v[:, sl]) + _dot_nt(q_dec[:, sl], st.astype(BF16)))
            st_ref[d, h] = st * dec[:, sl] + _dot_tn(v[:, sl], k_dec[:, sl])
        return rows, outs

    def first_touch(rows, outs):
        for h in range(heads):
            acc_ref[rows, h * HEAD_DIM:(h + 1) * HEAD_DIM] = outs[h]

    def finish(rows, outs):
        og = og_ref[0, rows, :].astype(F32)
        for h in range(heads):
            sl = slice(h * HEAD_DIM, (h + 1) * HEAD_DIM)
            o = outs[h] + acc_ref[rows, sl]
            gate = og[:, sl] * jax.nn.sigmoid(og[:, sl])
            o_ref[0, rows, sl] = (_rms(o, go_ref[...]) * gate).astype(o_ref.dtype)

    def body_first(i, carry):
        first_touch(*direction(i, 0))
        first_touch(*direction(n_chunks - 1 - i, 1))
        return carry

    def body_second(i, carry):
        finish(*direction(i, 0))
        finish(*direction(n_chunks - 1 - i, 1))
        return carry

    lax.fori_loop(0, half, body_first, 0)
    lax.fori_loop(half, n_chunks, body_second, 0)


def _hgrn(z3, lb, g_o, heads):
    B, S, _ = z3.shape
    assert (S // CHUNK) % 2 == 0 and HGRN_HEADS % heads == 0
    W = heads * HEAD_DIM
    nblk = HGRN_WIDTH // W
    base = FOURIER_WIDTH // W

    def zspec(j):
        return pl.BlockSpec((1, S, W), lambda b, h, j=j: (b, 0, base + j * nblk + h))

    return pl.pallas_call(
        functools.partial(_hgrn_kernel, seq=S, heads=heads),
        out_shape=jax.ShapeDtypeStruct((B, S, HGRN_WIDTH), BF16),
        grid=(B, nblk),
        in_specs=[zspec(0), zspec(1), zspec(2), zspec(3), zspec(4),
                  pl.BlockSpec((2, W), lambda b, h: (0, h)),
                  pl.BlockSpec((1, HEAD_DIM), lambda b, h: (0, 0))],
        out_specs=pl.BlockSpec((1, S, W), lambda b, h: (b, 0, h)),
        scratch_shapes=[pltpu.VMEM((S, W), F32),
                        pltpu.VMEM((2, heads, HEAD_DIM, HEAD_DIM), F32)],
        compiler_params=_params("parallel", "arbitrary"),
        name="hgrn",
    )(z3, z3, z3, z3, z3, lb, g_o)


R_E1, R_E2, R_W1, R_W2, R_RANK1, R_RANK2 = range(6)


def _outproj_kernel(x_ref, yf_ref, o_ref, wa_ref, wb_ref, g_ref, wr_ref, tri_ref,
                    x1_ref, h2_ref, route_ref, counts_ref, cnt_ref):
    @pl.when(pl.program_id(0) == 0)
    def _():
        cnt_ref[...] = jnp.zeros_like(cnt_ref)

    x1 = x_ref[...] + _dot(yf_ref[...], wa_ref[...]) + _dot(o_ref[...], wb_ref[...])
    x1_ref[...] = x1
    h2 = _rms(x1, g_ref[...])
    h2_ref[...] = h2.reshape(h2_ref.shape)
    logits = _dot(h2.astype(BF16), wr_ref[...])
    lane = lax.broadcasted_iota(jnp.int32, logits.shape, 1)
    neg = -jnp.inf

    def argmax_lowest(x, m):
        return jnp.min(jnp.where(x == m, lane, LANES), axis=1, keepdims=True)

    lg = jnp.where((lane >= N_EXPERTS) & (lane < N_EXPERTS + N_GROUPS), logits, neg)
    mg = jnp.max(lg, axis=1, keepdims=True)
    p_top = 1.0 / jnp.sum(jnp.exp(lg - mg), axis=1, keepdims=True)
    g_idx = argmax_lowest(lg, mg) - N_EXPERTS
    lo = g_idx * EXPERTS_PER_GROUP
    le = jnp.where((lane >= lo) & (lane < lo + EXPERTS_PER_GROUP), logits, neg)
    v1 = jnp.max(le, axis=1, keepdims=True)
    i1 = argmax_lowest(le, v1)
    le2 = jnp.where(lane == i1, neg, le)
    v2 = jnp.max(le2, axis=1, keepdims=True)
    i2 = argmax_lowest(le2, v2)
    e2 = jnp.exp(v2 - v1)
    w1 = p_top / (1.0 + e2)
    w2 = w1 * e2

    oh1 = jnp.where(lane == i1, 1.0, 0.0)
    oh2 = jnp.where(lane == i2, 1.0, 0.0)
    before1 = _dot(tri_ref[...], oh1.astype(BF16))
    before2 = _dot(tri_ref[...], oh2.astype(BF16))
    tot1 = jnp.sum(oh1, axis=0, keepdims=True)
    tot2 = jnp.sum(oh2, axis=0, keepdims=True)
    carry = cnt_ref[0:1, :]
    rank1 = jnp.sum(oh1 * (carry + before1), axis=1, keepdims=True)
    rank2 = jnp.sum(oh2 * (carry + tot1 + before2), axis=1, keepdims=True)
    new_counts = jnp.broadcast_to(carry + tot1 + tot2, cnt_ref.shape)
    cnt_ref[...] = new_counts
    counts_ref[...] = new_counts

    route = jnp.zeros(logits.shape, F32)
    for slot, val in ((R_E1, i1.astype(F32)), (R_E2, i2.astype(F32)), (R_W1, w1), (R_W2, w2),
                      (R_RANK1, rank1), (R_RANK2, rank2)):
        route = jnp.where(lane == slot, val, route)
    route_ref[...] = route


def _outproj(x2d, yf2d, o2d, wa, wb, g, wr, tile):
    T = x2d.shape[0]
    tri = jnp.tril(jnp.ones((tile, tile), F32), -1).astype(BF16)
    tok = lambda w: pl.BlockSpec((tile, w), lambda i: (i, 0))
    full = lambda a: pl.BlockSpec(a.shape, lambda i: (0,) * a.ndim)
    return pl.pallas_call(
        _outproj_kernel,
        out_shape=(jax.ShapeDtypeStruct((T, D_MODEL), F32),
                   jax.ShapeDtypeStruct((T,) + ROW, F32),
                   jax.ShapeDtypeStruct((T, LANES), F32),
                   jax.ShapeDtypeStruct((SUBLANES, LANES), F32)),
        grid=(T // tile,),
        in_specs=[tok(D_MODEL), tok(FOURIER_WIDTH), tok(HGRN_WIDTH), full(wa), full(wb), full(g), full(wr),
                  full(tri)],
        out_specs=(tok(D_MODEL), pl.BlockSpec((tile,) + ROW, lambda i: (i, 0, 0)), tok(LANES),
                   pl.BlockSpec((SUBLANES, LANES), lambda i: (0, 0))),
        scratch_shapes=[pltpu.VMEM((SUBLANES, LANES), F32)],
        compiler_params=_params("arbitrary"),
        name="outproj_router",
    )(x2d, yf2d, o2d, wa, wb, g, wr, tri)


def _row_copies(n, make):
    def body(t, carry):
        make(t).start()
        return carry
    lax.fori_loop(0, n, body, 0)


def _dispatch_kernel(fill_ref, pos1_ref, pos2_ref, h_ref, xs_hbm, zero_ref, sem):
    tile = h_ref.shape[0]

    @pl.when(pl.program_id(0) == 0)
    def _():
        zero_ref[...] = jnp.zeros_like(zero_ref)
        fill = lambda j: pltpu.make_async_copy(
            zero_ref, xs_hbm.at[pl.ds(pl.multiple_of(fill_ref[j] * tile, tile), tile)], sem.at[2])

        def start(j, carry):
            @pl.when(fill_ref[j] >= 0)
            def _():
                fill(j).start()
            return carry

        def wait(j, carry):
            @pl.when(fill_ref[j] >= 0)
            def _():
                fill(j).wait()
            return carry

        lax.fori_loop(0, fill_ref.shape[0], start, 0)
        lax.fori_loop(0, fill_ref.shape[0], wait, 0)

    for k, pos_ref in enumerate((pos1_ref, pos2_ref)):
        _row_copies(tile, lambda t, pos_ref=pos_ref, k=k:
                    pltpu.make_async_copy(h_ref.at[t], xs_hbm.at[pos_ref[t]], sem.at[k]))
    for k in range(2):
        pltpu.make_async_copy(h_ref, xs_hbm.at[pl.ds(0, tile)], sem.at[k]).wait()


def _dispatch(fill_tiles, pos1, pos2, h2r, n_rows, tile):
    T = h2r.shape[0]
    smem = pl.BlockSpec((tile,), lambda i, fill: (i,), memory_space=pltpu.SMEM)
    return pl.pallas_call(
        _dispatch_kernel,
        out_shape=jax.ShapeDtypeStruct((n_rows,) + ROW, F32),
        grid_spec=pltpu.PrefetchScalarGridSpec(
            num_scalar_prefetch=1, grid=(T // tile,),
            in_specs=[smem, smem, pl.BlockSpec((tile,) + ROW, lambda i, fill: (i, 0, 0))],
            out_specs=pl.BlockSpec(memory_space=pl.ANY),
            scratch_shapes=[pltpu.VMEM((tile,) + ROW, F32), pltpu.SemaphoreType.DMA((3,))]),
        compiler_params=_params("arbitrary"),
        name="moe_dispatch",
    )(fill_tiles, pos1, pos2, h2r)


def _expert_kernel(te_ref, tr_ref, xs_ref, wg_ref, wu_ref, wd_ref, ys_ref):
    i = pl.program_id(0)
    tile = xs_ref.shape[0]
    valid = tr_ref[i]

    @pl.when(valid > 0)
    def _():
        rows = lax.broadcasted_iota(jnp.int32, (tile, 1), 0)
        x = jnp.where(rows < valid, xs_ref[...].reshape(tile, D_MODEL), 0.0).astype(BF16)
        a = _dot(x, wg_ref[0])
        hid = (a * jax.nn.sigmoid(a) * _dot(x, wu_ref[0])).astype(BF16)
        ys_ref[...] = _dot(hid, wd_ref[0]).reshape(ys_ref.shape)

    @pl.when(valid == 0)
    def _():
        ys_ref[...] = jnp.zeros_like(ys_ref)


def _experts(tile_expert, tile_rows, xs, wg, wu, wd, tile):
    n_tiles = xs.shape[0] // tile
    wspec = lambda a: pl.BlockSpec((1,) + a.shape[1:], lambda i, te, tr: (te[i], 0, 0))
    rows = pl.BlockSpec((tile,) + ROW, lambda i, te, tr: (i, 0, 0))
    return pl.pallas_call(
        _expert_kernel,
        out_shape=jax.ShapeDtypeStruct(xs.shape, F32),
        grid_spec=pltpu.PrefetchScalarGridSpec(
            num_scalar_prefetch=2, grid=(n_tiles,),
            in_specs=[rows, wspec(wg), wspec(wu), wspec(wd)],
            out_specs=rows),
        compiler_params=_params("arbitrary"),
        name="moe_experts",
    )(tile_expert, tile_rows, xs, wg, wu, wd)


def _tail_kernel(pos1_ref, pos2_ref, ys_hbm, route_ref, x1_ref, p_ref, gp_ref, wpg_ref, wpp_ref, gf_ref,
                 out_ref, buf_ref, sem):
    tile = x1_ref.shape[0]
    for k, pos_ref in enumerate((pos1_ref, pos2_ref)):
        _row_copies(tile, lambda t, pos_ref=pos_ref, k=k:
                    pltpu.make_async_copy(ys_hbm.at[pos_ref[t]], buf_ref.at[k, t], sem.at[k]))
    ple = _dot(p_ref[...].astype(BF16), wpp_ref[...])
    lane = lax.broadcasted_iota(jnp.int32, route_ref.shape, 1)
    route = route_ref[...]
    y = None
    for k, slot in enumerate((R_W1, R_W2)):
        w = jnp.sum(jnp.where(lane == slot, route, 0.0), axis=1, keepdims=True)
        pltpu.make_async_copy(ys_hbm.at[pl.ds(0, tile)], buf_ref.at[k], sem.at[k]).wait()
        part = w * buf_ref[k].reshape(tile, D_MODEL)
        y = part if y is None else y + part
    x2 = x1_ref[...] + y
    hp = _rms(x2, gp_ref[...]).astype(BF16)
    gate = jax.nn.sigmoid(_dot(hp, wpg_ref[...]))
    out_ref[...] = _rms(x2 + ple * gate, gf_ref[...])


def _tail(pos1, pos2, ys, route, x1, p2d, gp, wpg, wpp, gf, tile):
    T = x1.shape[0]
    smem = pl.BlockSpec((tile,), lambda i: (i,), memory_space=pltpu.SMEM)
    tok = lambda w: pl.BlockSpec((tile, w), lambda i: (i, 0))
    full = lambda a: pl.BlockSpec(a.shape, lambda i: (0,) * a.ndim)
    return pl.pallas_call(
        _tail_kernel,
        out_shape=jax.ShapeDtypeStruct((T, D_MODEL), F32),
        grid=(T // tile,),
        in_specs=[smem, smem, pl.BlockSpec(memory_space=pl.ANY), tok(LANES), tok(D_MODEL), tok(PLE_DIM),
                  full(gp), full(wpg), full(wpp), full(gf)],
        out_specs=tok(D_MODEL),
        scratch_shapes=[pltpu.VMEM((2, tile) + ROW, F32), pltpu.SemaphoreType.DMA((2,))],
        compiler_params=_params("arbitrary"),
        name="combine_tail",
    )(pos1, pos2, ys, route, x1, p2d, gp, wpg, wpp, gf)


def _sorted_layout(route, counts, n_tiles, tile):
    cnt = counts[0, :N_EXPERTS].astype(jnp.int32)
    padded = (cnt + tile - 1) // tile * tile
    ends = jnp.cumsum(padded)
    starts = ends - padded
    e1 = route[:, R_E1].astype(jnp.int32)
    e2 = route[:, R_E2].astype(jnp.int32)
    pos1 = starts[e1] + route[:, R_RANK1].astype(jnp.int32)
    pos2 = starts[e2] + route[:, R_RANK2].astype(jnp.int32)
    tile_start = jnp.arange(n_tiles, dtype=jnp.int32) * tile
    tile_expert = jnp.minimum(jnp.searchsorted(ends, tile_start, side="right"), N_EXPERTS - 1).astype(jnp.int32)
    tile_rows = jnp.clip(cnt[tile_expert] - (tile_start - starts[tile_expert]), 0, tile)
    tile_rows = jnp.where(tile_start < ends[-1], tile_rows, 0).astype(jnp.int32)
    partial = jnp.where(padded > cnt, ends // tile - 1, -1)
    tail = ends[-1] // tile + jnp.arange(N_EXPERTS, dtype=jnp.int32)
    fill_tiles = jnp.concatenate([partial, jnp.where(tail < n_tiles, tail, -1)]).astype(jnp.int32)
    return pos1, pos2, tile_expert, tile_rows, fill_tiles


def _trunk(x, p, w):
    B, S, _ = x.shape
    T = B * S
    x2d = x.reshape(T, D_MODEL)
    z = _inproj(x2d, w["g_mix"], w["w_in"], tile=512)
    z3 = z.reshape(B, S, IN_WIDTH)
    yf = _fourier(z3, w["w_fourier"])
    o = _hgrn(z3, w["lb"], w["g_o"], heads=4 if S <= 2048 else 2)
    x1, h2r, route, counts = _outproj(x2d, yf.reshape(T, FOURIER_WIDTH), o.reshape(T, HGRN_WIDTH),
                                      w["w_out_a"], w["w_out_b"], w["g_ffn"], w["w_router"], tile=ROUTE_TILE)
    n_tiles = 2 * T // EXPERT_TILE + N_EXPERTS
    pos1, pos2, tile_expert, tile_rows, fill_tiles = _sorted_layout(route, counts, n_tiles, EXPERT_TILE)
    assert ROUTE_TILE == EXPERT_TILE, "dispatch zero-fills padding in units of its own tile"
    xs = _dispatch(fill_tiles, pos1, pos2, h2r, n_tiles * EXPERT_TILE, tile=ROUTE_TILE)
    ys = _experts(tile_expert, tile_rows, xs, w["w_eg"], w["w_eu"], w["w_ed"], tile=EXPERT_TILE)
    out = _tail(pos1, pos2, ys, route, x1, p.reshape(T, PLE_DIM), w["g_ple"], w["w_ple_gate"],
                w["w_ple_proj"], w["g_final"], tile=ROUTE_TILE)
    return out.reshape(B, S, D_MODEL)


def kernel(x_prompt, x_sample, p_prompt, p_sample, norm_mix, w_in, w_fourier, lb_logits, norm_o, w_out,
           norm_ffn, w_route_group, w_route_expert, w_exp_gate, w_exp_up, w_exp_down, norm_ple,
           w_ple_gate, w_ple_proj, norm_final):
    assert w_in.shape[0] == 1, "single-layer trunk"
    lb_all = jnp.cumsum(jax.nn.softmax(lb_logits.astype(F32), axis=0), axis=0)
    router = jnp.concatenate(
        [w_route_expert[0], w_route_group[0],
         jnp.zeros((D_MODEL, LANES - N_EXPERTS - N_GROUPS), F32)], axis=1)
    w = {
        "g_mix": norm_mix[0][None, :],
        "w_in": w_in[0].astype(BF16),
        "w_fourier": w_fourier[0],
        "lb": lb_all[0],
        "g_o": norm_o[0][None, :],
        "w_out_a": w_out[0, :FOURIER_WIDTH].astype(BF16),
        "w_out_b": w_out[0, FOURIER_WIDTH:].astype(BF16),
        "g_ffn": norm_ffn[0][None, :],
        "w_router": router.astype(BF16),
        "w_eg": w_exp_gate[0].astype(BF16),
        "w_eu": w_exp_up[0].astype(BF16),
        "w_ed": w_exp_down[0].astype(BF16),
        "g_ple": norm_ple[0][None, :],
        "w_ple_gate": w_ple_gate[0].astype(BF16),
        "w_ple_proj": w_ple_proj[0].astype(BF16),
        "g_final": norm_final[None, :],
    }
    return (_trunk(x_prompt, p_prompt[0], w), _trunk(x_sample, p_sample[0], w))
```

```python
import functools
import math

import numpy as np
import jax
import jax.numpy as jnp
from jax import lax
from jax.experimental import pallas as pl
from jax.experimental.pallas import tpu as pltpu

F32 = jnp.float32
BF16 = jnp.bfloat16
I32 = jnp.int32

D_MODEL = 1024
FOURIER_WIDTH = 512
FOURIER_GROUPS = 4
GROUP_DIM = 128
HGRN_WIDTH = 512
HEAD_DIM = 128
HGRN_HEADS = 4
CHUNK = 64
IN_WIDTH = FOURIER_WIDTH + 5 * HGRN_WIDTH
N_GROUPS = 4
EXPERTS_PER_GROUP = 8
N_EXPERTS = 32
PAIRS_PER_GROUP = EXPERTS_PER_GROUP * (EXPERTS_PER_GROUP - 1) // 2
N_CLASSES = N_GROUPS * PAIRS_PER_GROUP
D_EXPERT = 512
PLE_DIM = 256
EPS = 1e-6
LANES = 128
SUBLANES = 8
VMEM_LIMIT = 56 * 1024 * 1024
ROUTE_TILE = 512
EXPERT_TILE = 256
ROW = (SUBLANES, LANES)
SLAB = (2 * SUBLANES, LANES)
SLAB_WIDTH = 2 * D_MODEL
SLAB_ZEROS = (0,) * len(SLAB)
META_CLASS, META_RANK = 0, 1

assert N_CLASSES <= LANES


def _params(*sem):
    return pltpu.CompilerParams(dimension_semantics=sem, vmem_limit_bytes=VMEM_LIMIT)


def _dot(a, b):
    return jnp.dot(a, b, preferred_element_type=F32)


def _dot_nt(a, b):
    return lax.dot_general(a, b, (((1,), (1,)), ((), ())), preferred_element_type=F32)


def _dot_tn(a, b):
    return lax.dot_general(a, b, (((0,), (0,)), ((), ())), preferred_element_type=F32)


def _rms(x, g):
    return x * lax.rsqrt(jnp.mean(x * x, axis=-1, keepdims=True) + EPS) * g


def _inproj_kernel(x_ref, g_ref, w_ref, z_ref, h_ref):
    @pl.when(pl.program_id(1) == 0)
    def _():
        h_ref[...] = _rms(x_ref[...], g_ref[...]).astype(BF16)

    z_ref[...] = _dot(h_ref[...], w_ref[...]).astype(z_ref.dtype)


def _inproj(x2d, g, w_bf, tile):
    T = x2d.shape[0]
    tn = 1024
    return pl.pallas_call(
        _inproj_kernel,
        out_shape=jax.ShapeDtypeStruct((T, IN_WIDTH), BF16),
        grid=(T // tile, IN_WIDTH // tn),
        in_specs=[
            pl.BlockSpec((tile, D_MODEL), lambda i, j: (i, 0)),
            pl.BlockSpec((1, D_MODEL), lambda i, j: (0, 0)),
            pl.BlockSpec((D_MODEL, tn), lambda i, j: (0, j)),
        ],
        out_specs=pl.BlockSpec((tile, tn), lambda i, j: (i, j)),
        scratch_shapes=[pltpu.VMEM((tile, D_MODEL), BF16)],
        compiler_params=_params("parallel", "arbitrary"),
        name="inproj",
    )(x2d, g, w_bf)


def _fourier_fold_kernel(cc_ref, sc_ref, w_ref, m_ref, *, scale):
    for g in range(FOURIER_GROUPS):
        w = w_ref[g]
        m_ref[g, :GROUP_DIM, :] = (jnp.dot(cc_ref[...], w, preferred_element_type=F32,
                                           precision=lax.Precision.HIGHEST) * scale).astype(m_ref.dtype)
        m_ref[g, GROUP_DIM:, :] = (jnp.dot(sc_ref[...], w, preferred_element_type=F32,
                                           precision=lax.Precision.HIGHEST) * (-scale)).astype(m_ref.dtype)


def _fourier_fold(w_fourier, seq):
    cc, sc = _dft_tables(GROUP_DIM, F32)
    scale = 1.0 / math.sqrt(seq * GROUP_DIM)
    return pl.pallas_call(
        functools.partial(_fourier_fold_kernel, scale=scale),
        out_shape=jax.ShapeDtypeStruct((FOURIER_GROUPS, 2 * GROUP_DIM, GROUP_DIM), BF16),
        name="fourier_fold",
    )(cc, sc, w_fourier)


def _dft_tables(n, dtype):
    r = int(round(math.sqrt(n)))
    while n % r:
        r -= 1
    m = jnp.arange(n, dtype=I32)[None, :]
    k_lo = jnp.arange(r, dtype=I32)[:, None]
    k_hi = jnp.arange(n // r, dtype=I32)[:, None] * r
    step = 2.0 * math.pi / n
    a_lo = ((k_lo * m) % n).astype(F32) * step
    a_hi = ((k_hi * m) % n).astype(F32) * step
    c_lo, s_lo = jnp.cos(a_lo)[None], jnp.sin(a_lo)[None]
    c_hi, s_hi = jnp.cos(a_hi)[:, None], jnp.sin(a_hi)[:, None]
    cos = (c_hi * c_lo - s_hi * s_lo).reshape(n, n)
    sin = (s_hi * c_lo + c_hi * s_lo).reshape(n, n)
    return cos.astype(dtype), sin.astype(dtype)


def _fourier_kernel(u_ref, cos_ref, sin_ref, m_ref, y_ref):
    u = u_ref[0]
    a = _dot(cos_ref[...], u).astype(BF16)
    b = _dot(sin_ref[...], u).astype(BF16)
    for g in range(FOURIER_GROUPS):
        sl = slice(g * GROUP_DIM, (g + 1) * GROUP_DIM)
        ab = jnp.concatenate([a[:, sl], b[:, sl]], axis=1)
        y_ref[0, :, sl] = _dot(ab, m_ref[g]).astype(y_ref.dtype)


def _fourier(z3, w_fourier, tk=512):
    B, S, _ = z3.shape
    cos, sin = _dft_tables(S, BF16)
    m = _fourier_fold(w_fourier, S)
    return pl.pallas_call(
        _fourier_kernel,
        out_shape=jax.ShapeDtypeStruct((B, S, FOURIER_WIDTH), BF16),
        grid=(B, S // tk),
        in_specs=[
            pl.BlockSpec((1, S, FOURIER_WIDTH), lambda b, k: (b, 0, 0)),
            pl.BlockSpec((tk, S), lambda b, k: (k, 0)),
            pl.BlockSpec((tk, S), lambda b, k: (k, 0)),
            pl.BlockSpec((FOURIER_GROUPS, 2 * GROUP_DIM, GROUP_DIM), lambda b, k: (0, 0, 0)),
        ],
        out_specs=pl.BlockSpec((1, tk, FOURIER_WIDTH), lambda b, k: (b, k, 0)),
        compiler_params=_params("parallel", "arbitrary"),
        name="fourier",
    )(z3, cos, sin, m)


def _split3(x):
    hi = x.astype(BF16)
    r1 = x - hi.astype(F32)
    mid = r1.astype(BF16)
    lo = (r1 - mid.astype(F32)).astype(BF16)
    return hi, mid, lo


def _hgrn_kernel(q_ref, v_ref, ff_ref, fb_ref, og_ref, lb_ref, go_ref, o_ref, acc_ref, st_ref, *, seq, heads):
    n_chunks = seq // CHUNK
    half = n_chunks // 2
    row = lax.broadcasted_iota(I32, (CHUNK, CHUNK), 0)
    col = lax.broadcasted_iota(I32, (CHUNK, CHUNK), 1)
    masks = (row >= col, row <= col)
    tris = tuple(jnp.where(m, 1.0, 0.0).astype(BF16) for m in masks)
    edges = (CHUNK - 1, 0)
    f_refs = (ff_ref, fb_ref)
    inv_sqrt_d = HEAD_DIM ** -0.5

    st_ref[...] = jnp.zeros_like(st_ref)

    def direction(c, d):
        rows = pl.ds(pl.multiple_of(c * CHUNK, CHUNK), CHUNK)
        q = q_ref[0, rows, :].astype(F32)
        v = v_ref[0, rows, :]
        fr = f_refs[d][0, rows, :].astype(F32)
        lb = lb_ref[d:d + 1, :]
        f = lb + (1.0 - lb) * jax.nn.sigmoid(fr)
        k = (1.0 - lb) * jax.nn.sigmoid(-fr)
        hi, mid, lo = _split3(jnp.log(f))
        b = _dot(tris[d], hi) + _dot(tris[d], mid) + _dot(tris[d], lo)
        b_edge = b[edges[d]:edges[d] + 1, :]
        q_dec = (q * jnp.exp(b) * inv_sqrt_d).astype(BF16)
        k_inv = (k * jnp.exp(-b)).astype(BF16)
        k_dec = (k * jnp.exp(b_edge - b)).astype(BF16)
        dec = jnp.exp(b_edge)
        outs = []
        for h in range(heads):
            sl = slice(h * HEAD_DIM, (h + 1) * HEAD_DIM)
            st = st_ref[d, h]
            scores = jnp.where(masks[d], _dot_nt(q_dec[:, sl], k_inv[:, sl]), 0.0).astype(BF16)
            outs.append(_dot(scores, v[:, sl]) + _dot_nt(q_dec[:, sl], st.astype(BF16)))
            st_ref[d, h] = st * dec[:, sl] + _dot_tn(v[:, sl], k_dec[:, sl])
        return rows, outs

    def first_touch(rows, outs):
        for h in range(heads):
            acc_ref[rows, h * HEAD_DIM:(h + 1) * HEAD_DIM] = outs[h]

    def finish(rows, outs):
        og = og_ref[0, rows, :].astype(F32)
        for h in range(heads):
            sl = slice(h * HEAD_DIM, (h + 1) * HEAD_DIM)
            o = outs[h] + acc_ref[rows, sl]
            gate = og[:, sl] * jax.nn.sigmoid(og[:, sl])
            o_ref[0, rows, sl] = (_rms(o, go_ref[...]) * gate).astype(o_ref.dtype)

    def body_first(i, carry):
        first_touch(*direction(i, 0))
        first_touch(*direction(n_chunks - 1 - i, 1))
        return carry

    def body_second(i, carry):
        finish(*direction(i, 0))
        finish(*direction(n_chunks - 1 - i, 1))
        return carry

    lax.fori_loop(0, half, body_first, 0)
    lax.fori_loop(half, n_chunks, body_second, 0)


def _hgrn(z3, lb, g_o, heads):
    B, S, _ = z3.shape
    assert (S // CHUNK) % 2 == 0 and HGRN_HEADS % heads == 0
    W = heads * HEAD_DIM
    nblk = HGRN_WIDTH // W
    base = FOURIER_WIDTH // W

    def zspec(j):
        return pl.BlockSpec((1, S, W), lambda b, h, j=j: (b, 0, base + j * nblk + h))

    return pl.pallas_call(
        functools.partial(_hgrn_kernel, seq=S, heads=heads),
        out_shape=jax.ShapeDtypeStruct((B, S, HGRN_WIDTH), BF16),
        grid=(B, nblk),
        in_specs=[zspec(0), zspec(1), zspec(2), zspec(3), zspec(4),
                  pl.BlockSpec((2, W), lambda b, h: (0, h)),
                  pl.BlockSpec((1, HEAD_DIM), lambda b, h: (0, 0))],
        out_specs=pl.BlockSpec((1, S, W), lambda b, h: (b, 0, h)),
        scratch_shapes=[pltpu.VMEM((S, W), F32),
                        pltpu.VMEM((2, heads, HEAD_DIM, HEAD_DIM), F32)],
        compiler_params=_params("parallel", "arbitrary"),
        name="hgrn",
    )(z3, z3, z3, z3, z3, lb, g_o)


def _outproj_kernel(x_ref, yf_ref, o_ref, wa_ref, wb_ref, g_ref, wr_ref, tri_ref, cnt_in_ref,
                    x1_ref, slab_ref, meta_ref, counts_ref, cnt_ref):
    @pl.when(pl.program_id(0) == 0)
    def _():
        cnt_ref[...] = cnt_in_ref[...]

    tile = x_ref.shape[0]
    x1 = x_ref[...] + _dot(yf_ref[...], wa_ref[...]) + _dot(o_ref[...], wb_ref[...])
    x1_ref[...] = x1
    h2 = _rms(x1, g_ref[...])
    logits = _dot(h2.astype(BF16), wr_ref[...])
    lane = lax.broadcasted_iota(I32, logits.shape, 1)
    neg = -jnp.inf

    def argmax_lowest(x, m):
        return jnp.min(jnp.where(x == m, lane, LANES), axis=1, keepdims=True)

    lg = jnp.where((lane >= N_EXPERTS) & (lane < N_EXPERTS + N_GROUPS), logits, neg)
    mg = jnp.max(lg, axis=1, keepdims=True)
    p_top = 1.0 / jnp.sum(jnp.exp(lg - mg), axis=1, keepdims=True)
    g_idx = argmax_lowest(lg, mg) - N_EXPERTS
    base = g_idx * EXPERTS_PER_GROUP
    le = jnp.where((lane >= base) & (lane < base + EXPERTS_PER_GROUP), logits, neg)
    v1 = jnp.max(le, axis=1, keepdims=True)
    i1 = argmax_lowest(le, v1)
    le2 = jnp.where(lane == i1, neg, le)
    v2 = jnp.max(le2, axis=1, keepdims=True)
    i2 = argmax_lowest(le2, v2)
    e2 = jnp.exp(v2 - v1)
    w1 = p_top / (1.0 + e2)
    w2 = w1 * e2

    a = jnp.minimum(i1, i2) - base
    b = jnp.maximum(i1, i2) - base
    pair = lax.shift_right_logical(a * (2 * EXPERTS_PER_GROUP - 1 - a), 1) + (b - a - 1)
    cls = g_idx * PAIRS_PER_GROUP + pair
    first_is_lo = i1 < i2
    w_lo = jnp.where(first_is_lo, w1, w2)
    w_hi = jnp.where(first_is_lo, w2, w1)

    onehot = jnp.where(lane == cls, 1.0, 0.0)
    before = _dot(tri_ref[...], onehot.astype(BF16))
    carry = cnt_ref[0:1, :]
    rank = jnp.sum(onehot * (carry + before), axis=1, keepdims=True)
    new_counts = jnp.broadcast_to(carry + jnp.sum(onehot, axis=0, keepdims=True), cnt_ref.shape)
    cnt_ref[...] = new_counts
    counts_ref[...] = new_counts

    slab = jnp.concatenate([h2, jnp.broadcast_to(w_lo, (tile, LANES)), jnp.broadcast_to(w_hi, (tile, LANES)),
                            jnp.zeros((tile, SLAB_WIDTH - D_MODEL - 2 * LANES), F32)], axis=1)
    slab_ref[...] = slab.reshape((tile,) + SLAB)

    meta = jnp.where(lane == META_CLASS, cls, jnp.where(lane == META_RANK, rank.astype(I32), 0))
    meta_ref[...] = meta.T[0:SUBLANES, :]


def _outproj(x2d, yf2d, o2d, wa, wb, g, wr, counts_in, tile):
    T = x2d.shape[0]
    tri = jnp.tril(jnp.ones((tile, tile), F32), -1).astype(BF16)
    tok = lambda w: pl.BlockSpec((tile, w), lambda i: (i, 0))
    full = lambda a: pl.BlockSpec(a.shape, lambda i: (0,) * a.ndim)
    return pl.pallas_call(
        _outproj_kernel,
        out_shape=(jax.ShapeDtypeStruct((T, D_MODEL), F32),
                   jax.ShapeDtypeStruct((T,) + SLAB, F32),
                   jax.ShapeDtypeStruct((SUBLANES, T), I32),
                   jax.ShapeDtypeStruct((SUBLANES, LANES), F32)),
        grid=(T // tile,),
        in_specs=[tok(D_MODEL), tok(FOURIER_WIDTH), tok(HGRN_WIDTH), full(wa), full(wb), full(g), full(wr),
                  full(tri), full(counts_in)],
        out_specs=(tok(D_MODEL), pl.BlockSpec((tile,) + SLAB, lambda i: (i,) + SLAB_ZEROS),
                   pl.BlockSpec((SUBLANES, tile), lambda i: (0, i)),
                   pl.BlockSpec((SUBLANES, LANES), lambda i: (0, 0))),
        scratch_shapes=[pltpu.VMEM((SUBLANES, LANES), F32)],
        compiler_params=_params("arbitrary"),
        name="outproj_router",
    )(x2d, yf2d, o2d, wa, wb, g, wr, tri, counts_in)


def _sorted_row(starts_ref, meta_ref, t):
    return starts_ref[meta_ref[META_CLASS, t]] + meta_ref[META_RANK, t]


def _start_row_copies(n, make):
    def body(t, carry):
        make(t).start()
        return carry
    lax.fori_loop(0, n, body, 0)


def _dispatch_kernel(starts_ref, fill_ref, *refs, bounds):
    n_trunks = len(bounds) - 1
    meta_refs, slab_refs = refs[0:2 * n_trunks:2], refs[1:2 * n_trunks:2]
    xs_hbm, zero_ref, sem = refs[2 * n_trunks:]
    i = pl.program_id(0)
    ftile = zero_ref.shape[0]

    @pl.when(i == 0)
    def _():
        zero_ref[...] = jnp.zeros_like(zero_ref)
        fill = lambda j: pltpu.make_async_copy(
            zero_ref, xs_hbm.at[pl.ds(pl.multiple_of(fill_ref[j] * ftile, ftile), ftile)], sem.at[1])

        def start(j, carry):
            @pl.when(fill_ref[j] >= 0)
            def _():
                fill(j).start()
            return carry

        def wait(j, carry):
            @pl.when(fill_ref[j] >= 0)
            def _():
                fill(j).wait()
            return carry

        lax.fori_loop(0, fill_ref.shape[0], start, 0)
        lax.fori_loop(0, fill_ref.shape[0], wait, 0)

    for k in range(n_trunks):
        @pl.when((i >= bounds[k]) & (i < bounds[k + 1]))
        def _(meta_ref=meta_refs[k], slab_ref=slab_refs[k]):
            tile = slab_ref.shape[0]
            _start_row_copies(tile, lambda t: pltpu.make_async_copy(
                slab_ref.at[t], xs_hbm.at[_sorted_row(starts_ref, meta_ref, t)], sem.at[0]))
            pltpu.make_async_copy(slab_ref, xs_hbm.at[pl.ds(0, tile)], sem.at[0]).wait()


def _dispatch(starts, fill_tiles, metas, slabs, n_rows, tile):
    steps = [slab.shape[0] // tile for slab in slabs]
    bounds = tuple(int(b) for b in np.cumsum([0] + steps))
    in_specs, args = [], []
    for k, (meta, slab) in enumerate(zip(metas, slabs)):
        local = lambda i, k=k: jnp.clip(i - bounds[k], 0, steps[k] - 1)
        in_specs.append(pl.BlockSpec((SUBLANES, tile), lambda i, s, f, local=local: (0, local(i)),
                                     memory_space=pltpu.SMEM))
        in_specs.append(pl.BlockSpec((tile,) + SLAB, lambda i, s, f, local=local: (local(i),) + SLAB_ZEROS))
        args += [meta, slab]
    return pl.pallas_call(
        functools.partial(_dispatch_kernel, bounds=bounds),
        out_shape=jax.ShapeDtypeStruct((n_rows,) + SLAB, F32),
        grid_spec=pltpu.PrefetchScalarGridSpec(
            num_scalar_prefetch=2, grid=(bounds[-1],),
            in_specs=in_specs,
            out_specs=pl.BlockSpec(memory_space=pl.ANY),
            scratch_shapes=[pltpu.VMEM((EXPERT_TILE,) + SLAB, F32), pltpu.SemaphoreType.DMA((2,))]),
        compiler_params=_params("arbitrary"),
        name="moe_dispatch",
    )(starts, fill_tiles, *args)


def _expert_kernel(elo_ref, ehi_ref, rows_ref, xs_ref, wg0_ref, wu0_ref, wd0_ref, wg1_ref, wu1_ref, wd1_ref,
                   ys_ref):
    i = pl.program_id(0)
    tile = xs_ref.shape[0]
    valid = rows_ref[i]

    @pl.when(valid > 0)
    def _():
        slab = xs_ref[...].reshape(tile, SLAB_WIDTH)
        x = slab[:, :D_MODEL].astype(BF16)
        y = None
        for k, (wg_ref, wu_ref, wd_ref) in enumerate(((wg0_ref, wu0_ref, wd0_ref), (wg1_ref, wu1_ref, wd1_ref))):
            a = _dot(x, wg_ref[0])
            hid = (a * jax.nn.sigmoid(a) * _dot(x, wu_ref[0])).astype(BF16)
            weight = jnp.tile(slab[:, D_MODEL + k * LANES:D_MODEL + (k + 1) * LANES], (1, D_MODEL // LANES))
            part = weight * _dot(hid, wd_ref[0])
            y = part if y is None else y + part
        ys_ref[...] = y.reshape(ys_ref.shape)

    @pl.when(valid == 0)
    def _():
        ys_ref[...] = jnp.zeros_like(ys_ref)


def _experts(tile_elo, tile_ehi, tile_rows, xs, wg, wu, wd, tile):
    n_tiles = xs.shape[0] // tile
    lo = lambda a: pl.BlockSpec((1,) + a.shape[1:], lambda i, elo, ehi, rows: (elo[i], 0, 0))
    hi = lambda a: pl.BlockSpec((1,) + a.shape[1:], lambda i, elo, ehi, rows: (ehi[i], 0, 0))
    return pl.pallas_call(
        _expert_kernel,
        out_shape=jax.ShapeDtypeStruct((xs.shape[0],) + ROW, F32),
        grid_spec=pltpu.PrefetchScalarGridSpec(
            num_scalar_prefetch=3, grid=(n_tiles,),
            in_specs=[pl.BlockSpec((tile,) + SLAB, lambda i, elo, ehi, rows: (i,) + SLAB_ZEROS),
                      lo(wg), lo(wu), lo(wd), hi(wg), hi(wu), hi(wd)],
            out_specs=pl.BlockSpec((tile,) + ROW, lambda i, elo, ehi, rows: (i, 0, 0))),
        compiler_params=_params("arbitrary"),
        name="moe_experts",
    )(tile_elo, tile_ehi, tile_rows, xs, wg, wu, wd, wg, wu, wd)


def _tail_kernel(starts_ref, meta_ref, meta_next_ref, ys_hbm, x1_ref, p_ref, gp_ref, wpg_ref, wpp_ref, gf_ref,
                 out_ref, buf_ref, sem, *, n):
    i = pl.program_id(0)
    tile = x1_ref.shape[0]

    def gather(m_ref, slot):
        _start_row_copies(tile, lambda t: pltpu.make_async_copy(
            ys_hbm.at[_sorted_row(starts_ref, m_ref, t)], buf_ref.at[slot, t], sem.at[slot]))

    @pl.when(i == 0)
    def _():
        gather(meta_ref, 0)

    @pl.when(i + 1 < n)
    def _():
        gather(meta_next_ref, (i + 1) % 2)

    ple = _dot(p_ref[...].astype(BF16), wpp_ref[...])
    slot = i % 2
    pltpu.make_async_copy(ys_hbm.at[pl.ds(0, tile)], buf_ref.at[slot], sem.at[slot]).wait()
    x2 = x1_ref[...] + buf_ref[slot].reshape(tile, D_MODEL)
    hp = _rms(x2, gp_ref[...]).astype(BF16)
    gate = jax.nn.sigmoid(_dot(hp, wpg_ref[...]))
    out_ref[...] = _rms(x2 + ple * gate, gf_ref[...])


def _tail(starts, meta, ys, x1, p2d, gp, wpg, wpp, gf, tile):
    T = x1.shape[0]
    n = T // tile
    tok = lambda w: pl.BlockSpec((tile, w), lambda i, s: (i, 0))
    full = lambda a: pl.BlockSpec(a.shape, lambda i, s: (0,) * a.ndim)
    return pl.pallas_call(
        functools.partial(_tail_kernel, n=n),
        out_shape=jax.ShapeDtypeStruct((T, D_MODEL), F32),
        grid_spec=pltpu.PrefetchScalarGridSpec(
            num_scalar_prefetch=1, grid=(n,),
            in_specs=[pl.BlockSpec((SUBLANES, tile), lambda i, s: (0, i), memory_space=pltpu.SMEM),
                      pl.BlockSpec((SUBLANES, tile), lambda i, s: (0, jnp.minimum(i + 1, n - 1)),
                                   memory_space=pltpu.SMEM),
                      pl.BlockSpec(memory_space=pl.ANY), tok(D_MODEL), tok(PLE_DIM),
                      full(gp), full(wpg), full(wpp), full(gf)],
            out_specs=tok(D_MODEL),
            scratch_shapes=[pltpu.VMEM((2, tile) + ROW, F32), pltpu.SemaphoreType.DMA((2,))]),
        compiler_params=_params("arbitrary"),
        name="combine_tail",
    )(starts, meta, meta, ys, x1, p2d, gp, wpg, wpp, gf)


def _class_experts():
    lo, hi = [], []
    for g in range(N_GROUPS):
        for a in range(EXPERTS_PER_GROUP):
            for b in range(a + 1, EXPERTS_PER_GROUP):
                lo.append(g * EXPERTS_PER_GROUP + a)
                hi.append(g * EXPERTS_PER_GROUP + b)
    return np.asarray(lo, np.int32), np.asarray(hi, np.int32)


def _sorted_layout(counts, n_tiles, tile):
    cnt = counts[0, :N_CLASSES].astype(I32)
    padded = (cnt + tile - 1) // tile * tile
    ends = jnp.cumsum(padded)
    starts = ends - padded
    tile_start = jnp.arange(n_tiles, dtype=I32) * tile
    tile_class = jnp.minimum(jnp.sum((ends[None, :] <= tile_start[:, None]).astype(I32), axis=1), N_CLASSES - 1)
    onehot = (tile_class[:, None] == jnp.arange(N_CLASSES, dtype=I32)[None, :]).astype(I32)
    pick = lambda table: jnp.sum(onehot * table[None, :], axis=1).astype(I32)
    tile_rows = jnp.clip(pick(cnt) - (tile_start - pick(starts)), 0, tile)
    tile_rows = jnp.where(tile_start < ends[-1], tile_rows, 0).astype(I32)
    class_lo, class_hi = _class_experts()
    partial = jnp.where(padded > cnt, ends // tile - 1, -1)
    tail = ends[-1] // tile + jnp.arange(N_CLASSES, dtype=I32)
    fill_tiles = jnp.concatenate([partial, jnp.where(tail < n_tiles, tail, -1)]).astype(I32)
    starts_padded = jnp.zeros((LANES,), I32).at[:N_CLASSES].set(starts)
    return starts_padded, pick(jnp.asarray(class_lo)), pick(jnp.asarray(class_hi)), tile_rows, fill_tiles


def _mix_and_route(x, w, counts_in):
    B, S, _ = x.shape
    T = B * S
    x2d = x.reshape(T, D_MODEL)
    z = _inproj(x2d, w["g_mix"], w["w_in"], tile=512)
    z3 = z.reshape(B, S, IN_WIDTH)
    yf = _fourier(z3, w["w_fourier"])
    o = _hgrn(z3, w["lb"], w["g_o"], heads=4 if S <= 2048 else 2)
    return _outproj(x2d, yf.reshape(T, FOURIER_WIDTH), o.reshape(T, HGRN_WIDTH),
                    w["w_out_a"], w["w_out_b"], w["g_ffn"], w["w_router"], counts_in, tile=ROUTE_TILE)


def kernel(x_prompt, x_sample, p_prompt, p_sample, norm_mix, w_in, w_fourier, lb_logits, norm_o, w_out,
           norm_ffn, w_route_group, w_route_expert, w_exp_gate, w_exp_up, w_exp_down, norm_ple,
           w_ple_gate, w_ple_proj, norm_final):
    assert w_in.shape[0] == 1, "single-layer trunk"
    lb_all = jnp.cumsum(jax.nn.softmax(lb_logits.astype(F32), axis=0), axis=0)
    router = jnp.concatenate(
        [w_route_expert[0], w_route_group[0],
         jnp.zeros((D_MODEL, LANES - N_EXPERTS - N_GROUPS), F32)], axis=1)
    w = {
        "g_mix": norm_mix[0][None, :],
        "w_in": w_in[0].astype(BF16),
        "w_fourier": w_fourier[0],
        "lb": lb_all[0],
        "g_o": norm_o[0][None, :],
        "w_out_a": w_out[0, :FOURIER_WIDTH].astype(BF16),
        "w_out_b": w_out[0, FOURIER_WIDTH:].astype(BF16),
        "g_ffn": norm_ffn[0][None, :],
        "w_router": router.astype(BF16),
    }
    xs_in = (x_prompt, x_sample)
    ps_in = (p_prompt[0], p_sample[0])

    counts = jnp.zeros((SUBLANES, LANES), F32)
    routed = []
    for x in xs_in:
        x1, slab, meta, counts = _mix_and_route(x, w, counts)
        routed.append((x1, slab, meta))

    total = sum(x.shape[0] * x.shape[1] for x in xs_in)
    n_tiles = total // EXPERT_TILE + N_CLASSES
    starts, tile_elo, tile_ehi, tile_rows, fill_tiles = _sorted_layout(counts, n_tiles, EXPERT_TILE)

    xs = _dispatch(starts, fill_tiles, [r[2] for r in routed], [r[1] for r in routed],
                   n_tiles * EXPERT_TILE, tile=ROUTE_TILE)
    ys = _experts(tile_elo, tile_ehi, tile_rows, xs, w_exp_gate[0].astype(BF16), w_exp_up[0].astype(BF16),
                  w_exp_down[0].astype(BF16), tile=EXPERT_TILE)

    outs = []
    for (x1, _, meta), x, p in zip(routed, xs_in, ps_in):
        out = _tail(starts, meta, ys, x1, p.reshape(-1, PLE_DIM), norm_ple[0][None, :],
                    w_ple_gate[0].astype(BF16), w_ple_proj[0].astype(BF16), norm_final[None, :],
                    tile=ROUTE_TILE)
        outs.append(out.reshape(x.shape))
    return tuple(outs)
```

```python
import functools
import math

import numpy as np
import jax
import jax.numpy as jnp
from jax import lax
from jax.experimental import pallas as pl
from jax.experimental.pallas import tpu as pltpu

F32 = jnp.float32
BF16 = jnp.bfloat16
I32 = jnp.int32

D_MODEL = 1024
FOURIER_WIDTH = 512
FOURIER_GROUPS = 4
GROUP_DIM = 128
HGRN_WIDTH = 512
HEAD_DIM = 128
HGRN_HEADS = 4
CHUNK = 64
IN_WIDTH = FOURIER_WIDTH + 5 * HGRN_WIDTH
N_GROUPS = 4
EXPERTS_PER_GROUP = 8
N_EXPERTS = 32
PAIRS_PER_GROUP = EXPERTS_PER_GROUP * (EXPERTS_PER_GROUP - 1) // 2
N_CLASSES = N_GROUPS * PAIRS_PER_GROUP
D_EXPERT = 512
PLE_DIM = 256
EPS = 1e-6
LANES = 128
SUBLANES = 8
VMEM_LIMIT = 56 * 1024 * 1024
ROUTE_TILE = 512
EXPERT_TILE = 256
ROW = (SUBLANES, LANES)
SLAB = (2 * SUBLANES, LANES)
SLAB_WIDTH = 2 * D_MODEL
SLAB_ZEROS = (0,) * len(SLAB)
META_CLASS, META_RANK = 0, 1

assert N_CLASSES <= LANES


def _params(*sem):
    return pltpu.CompilerParams(dimension_semantics=sem, vmem_limit_bytes=VMEM_LIMIT)


def _dot(a, b):
    return jnp.dot(a, b, preferred_element_type=F32)


def _dot_nt(a, b):
    return lax.dot_general(a, b, (((1,), (1,)), ((), ())), preferred_element_type=F32)


def _dot_tn(a, b):
    return lax.dot_general(a, b, (((0,), (0,)), ((), ())), preferred_element_type=F32)


def _rms(x, g):
    return x * lax.rsqrt(jnp.mean(x * x, axis=-1, keepdims=True) + EPS) * g


def _inproj_kernel(x_ref, g_ref, w_ref, z_ref):
    h = _rms(x_ref[...], g_ref[...]).astype(BF16)
    for j in range(0, IN_WIDTH, D_MODEL):
        z_ref[:, j:j + D_MODEL] = _dot(h, w_ref[:, j:j + D_MODEL]).astype(z_ref.dtype)


def _inproj(x2d, g, w_bf, tile):
    T = x2d.shape[0]
    return pl.pallas_call(
        _inproj_kernel,
        out_shape=jax.ShapeDtypeStruct((T, IN_WIDTH), BF16),
        grid=(T // tile,),
        in_specs=[
            pl.BlockSpec((tile, D_MODEL), lambda i: (i, 0)),
            pl.BlockSpec((1, D_MODEL), lambda i: (0, 0)),
            pl.BlockSpec((D_MODEL, IN_WIDTH), lambda i: (0, 0)),
        ],
        out_specs=pl.BlockSpec((tile, IN_WIDTH), lambda i: (i, 0)),
        compiler_params=_params("parallel"),
        name="inproj",
    )(x2d, g, w_bf)


def _fourier_fold_kernel(cc_ref, sc_ref, w_ref, m_ref, *, scale):
    for g in range(FOURIER_GROUPS):
        w = w_ref[g]
        m_ref[g, :GROUP_DIM, :] = (jnp.dot(cc_ref[...], w, preferred_element_type=F32,
                                           precision=lax.Precision.HIGHEST) * scale).astype(m_ref.dtype)
        m_ref[g, GROUP_DIM:, :] = (jnp.dot(sc_ref[...], w, preferred_element_type=F32,
                                           precision=lax.Precision.HIGHEST) * (-scale)).astype(m_ref.dtype)


def _fourier_fold(w_fourier, seq):
    cc, sc = _dft_tables(GROUP_DIM, GROUP_DIM, 1, 0, GROUP_DIM, F32)
    scale = 1.0 / math.sqrt(seq * GROUP_DIM)
    return pl.pallas_call(
        functools.partial(_fourier_fold_kernel, scale=scale),
        out_shape=jax.ShapeDtypeStruct((FOURIER_GROUPS, 2 * GROUP_DIM, GROUP_DIM), BF16),
        name="fourier_fold",
    )(cc, sc, w_fourier)


def _dft_tables(n_rows, n_cols, row_mult, row_off, period, dtype):
    r = int(round(math.sqrt(n_rows)))
    while n_rows % r:
        r -= 1
    m = jnp.arange(n_cols, dtype=I32)[None, :]
    k_lo = jnp.arange(r, dtype=I32)[:, None] * row_mult + row_off
    k_hi = jnp.arange(n_rows // r, dtype=I32)[:, None] * (r * row_mult)
    step = 2.0 * math.pi / period
    a_lo = (((k_lo % period) * m) % period).astype(F32) * step
    a_hi = (((k_hi % period) * m) % period).astype(F32) * step
    c_lo, s_lo = jnp.cos(a_lo)[None], jnp.sin(a_lo)[None]
    c_hi, s_hi = jnp.cos(a_hi)[:, None], jnp.sin(a_hi)[:, None]
    cos = (c_hi * c_lo - s_hi * s_lo).reshape(n_rows, n_cols)
    sin = (s_hi * c_lo + c_hi * s_lo).reshape(n_rows, n_cols)
    return cos.astype(dtype), sin.astype(dtype)


def _fourier_kernel(ulo_ref, uhi_ref, ce_ref, se_ref, co_ref, so_ref, m_ref, y_ref, sum_ref, dif_ref, stage_ref):
    tk = ce_ref.shape[0]

    @pl.when(pl.program_id(1) == 0)
    def _():
        lo = ulo_ref[0].astype(F32)
        hi = uhi_ref[0].astype(F32)
        sum_ref[...] = (lo + hi).astype(BF16)
        dif_ref[...] = (lo - hi).astype(BF16)

    def mix(cos_ref, sin_ref, x, parity):
        a = _dot(cos_ref[...], x).astype(BF16)
        b = _dot(sin_ref[...], x).astype(BF16)
        for g in range(FOURIER_GROUPS):
            sl = slice(g * GROUP_DIM, (g + 1) * GROUP_DIM)
            stage_ref[g, pl.ds(parity, tk, stride=2), :] = _dot(
                jnp.concatenate([a[:, sl], b[:, sl]], axis=1), m_ref[g])

    mix(ce_ref, se_ref, sum_ref[...], 0)
    mix(co_ref, so_ref, dif_ref[...], 1)
    for g in range(FOURIER_GROUPS):
        y_ref[0, :, g * GROUP_DIM:(g + 1) * GROUP_DIM] = stage_ref[g].astype(y_ref.dtype)


def _fourier(z3, w_fourier, tk=512):
    B, S, _ = z3.shape
    M = S // 2
    tk = min(tk, M)
    ce, se = _dft_tables(M, M, 1, 0, M, BF16)
    co, so = _dft_tables(M, M, 2, 1, S, BF16)
    m = _fourier_fold(w_fourier, S)
    table = pl.BlockSpec((tk, M), lambda b, k: (k, 0))
    return pl.pallas_call(
        _fourier_kernel,
        out_shape=jax.ShapeDtypeStruct((B, S, FOURIER_WIDTH), BF16),
        grid=(B, M // tk),
        in_specs=[
            pl.BlockSpec((1, M, FOURIER_WIDTH), lambda b, k: (b, 0, 0)),
            pl.BlockSpec((1, M, FOURIER_WIDTH), lambda b, k: (b, 1, 0)),
            table, table, table, table,
            pl.BlockSpec((FOURIER_GROUPS, 2 * GROUP_DIM, GROUP_DIM), lambda b, k: (0, 0, 0)),
        ],
        out_specs=pl.BlockSpec((1, 2 * tk, FOURIER_WIDTH), lambda b, k: (b, k, 0)),
        scratch_shapes=[pltpu.VMEM((M, FOURIER_WIDTH), BF16), pltpu.VMEM((M, FOURIER_WIDTH), BF16),
                        pltpu.VMEM((FOURIER_GROUPS, 2 * tk, GROUP_DIM), F32)],
        compiler_params=_params("parallel", "arbitrary"),
        name="fourier",
    )(z3, z3, ce, se, co, so, m)


def _split3(x):
    hi = x.astype(BF16)
    r1 = x - hi.astype(F32)
    mid = r1.astype(BF16)
    lo = (r1 - mid.astype(F32)).astype(BF16)
    return hi, mid, lo


def _hgrn_kernel(q_ref, v_ref, ff_ref, fb_ref, og_ref, lb_ref, go_ref, o_ref,
                 acc_ref, st_ref, qd_ref, ki_ref, kd_ref, dec_ref, *, seq, heads):
    n_chunks = seq // CHUNK
    half = n_chunks // 2
    assert half % 2 == 0
    row = lax.broadcasted_iota(I32, (CHUNK, CHUNK), 0)
    col = lax.broadcasted_iota(I32, (CHUNK, CHUNK), 1)
    masks = (row >= col, row <= col)
    tris = tuple(jnp.where(m, 1.0, 0.0).astype(BF16) for m in masks)
    edges = (CHUNK - 1, 0)
    f_refs = (ff_ref, fb_ref)
    inv_sqrt_d = HEAD_DIM ** -0.5

    st_ref[...] = jnp.zeros_like(st_ref)

    def chunk_rows(i, d):
        c = i if d == 0 else n_chunks - 1 - i
        return pl.ds(pl.multiple_of(c * CHUNK, CHUNK), CHUNK)

    def prepare(i, d, slot):
        rows = chunk_rows(jnp.minimum(i, n_chunks - 1), d)
        q = q_ref[0, rows, :].astype(F32)
        fr = f_refs[d][0, rows, :].astype(F32)
        lb = lb_ref[d:d + 1, :]
        f = lb + (1.0 - lb) * jax.nn.sigmoid(fr)
        k = 1.0 - f
        hi, mid, lo = _split3(jnp.log(f))
        b = _dot(tris[d], hi) + _dot(tris[d], mid) + _dot(tris[d], lo)
        b_edge = b[edges[d]:edges[d] + 1, :]
        qd_ref[d, slot] = (q * jnp.exp(b) * inv_sqrt_d).astype(BF16)
        ki_ref[d, slot] = (k * jnp.exp(-b)).astype(BF16)
        kd_ref[d, slot] = (k * jnp.exp(b_edge - b)).astype(BF16)
        dec_ref[d, slot] = jnp.broadcast_to(jnp.exp(b_edge), dec_ref.shape[2:])

    def advance(i, d, slot):
        rows = chunk_rows(i, d)
        v = v_ref[0, rows, :]
        q_dec, k_inv, k_dec = qd_ref[d, slot], ki_ref[d, slot], kd_ref[d, slot]
        dec = dec_ref[d, slot, 0:1, :]
        outs = []
        for h in range(heads):
            sl = slice(h * HEAD_DIM, (h + 1) * HEAD_DIM)
            st = st_ref[d, h]
            scores = jnp.where(masks[d], _dot_nt(q_dec[:, sl], k_inv[:, sl]), 0.0).astype(BF16)
            outs.append(_dot(scores, v[:, sl]) + _dot_nt(q_dec[:, sl], st.astype(BF16)))
            st_ref[d, h] = st * dec[:, sl] + _dot_tn(v[:, sl], k_dec[:, sl])
        return rows, outs

    def first_touch(rows, outs):
        for h in range(heads):
            acc_ref[rows, h * HEAD_DIM:(h + 1) * HEAD_DIM] = outs[h]

    def finish(rows, outs):
        og = og_ref[0, rows, :].astype(F32)
        for h in range(heads):
            sl = slice(h * HEAD_DIM, (h + 1) * HEAD_DIM)
            o = outs[h] + acc_ref[rows, sl]
            gate = og[:, sl] * jax.nn.sigmoid(og[:, sl])
            o_ref[0, rows, sl] = (_rms(o, go_ref[...]) * gate).astype(o_ref.dtype)

    def two_steps(sink):
        def body(j, carry):
            i = 2 * j
            for slot in (0, 1):
                for d in (0, 1):
                    prepare(i + slot + 1, d, 1 - slot)
                for d in (0, 1):
                    sink(*advance(i + slot, d, slot))
            return carry
        return body

    for d in (0, 1):
        prepare(0, d, 0)
    lax.fori_loop(0, half // 2, two_steps(first_touch), 0)
    lax.fori_loop(half // 2, n_chunks // 2, two_steps(finish), 0)


def _hgrn(z3, lb, g_o, heads):
    B, S, _ = z3.shape
    assert (S // CHUNK) % 2 == 0 and HGRN_HEADS % heads == 0
    W = heads * HEAD_DIM
    nblk = HGRN_WIDTH // W
    base = FOURIER_WIDTH // W

    def zspec(j):
        return pl.BlockSpec((1, S, W), lambda b, h, j=j: (b, 0, base + j * nblk + h))

    return pl.pallas_call(
        functools.partial(_hgrn_kernel, seq=S, heads=heads),
        out_shape=jax.ShapeDtypeStruct((B, S, HGRN_WIDTH), BF16),
        grid=(B, nblk),
        in_specs=[zspec(0), zspec(1), zspec(2), zspec(3), zspec(4),
                  pl.BlockSpec((2, W), lambda b, h: (0, h)),
                  pl.BlockSpec((1, HEAD_DIM), lambda b, h: (0, 0))],
        out_specs=pl.BlockSpec((1, S, W), lambda b, h: (b, 0, h)),
        scratch_shapes=[pltpu.VMEM((S, W), F32),
                        pltpu.VMEM((2, heads, HEAD_DIM, HEAD_DIM), F32),
                        pltpu.VMEM((2, 2, CHUNK, W), BF16),
                        pltpu.VMEM((2, 2, CHUNK, W), BF16),
                        pltpu.VMEM((2, 2, CHUNK, W), BF16),
                        pltpu.VMEM((2, 2, SUBLANES, W), F32)],
        compiler_params=_params("parallel", "arbitrary"),
        name="hgrn",
    )(z3, z3, z3, z3, z3, lb, g_o)


def _outproj_kernel(x_ref, yf_ref, o_ref, wa_ref, wb_ref, g_ref, wr_ref, tri_ref, cnt_in_ref,
                    x1_ref, slab_ref, meta_ref, counts_ref, cnt_ref):
    @pl.when(pl.program_id(0) == 0)
    def _():
        cnt_ref[...] = cnt_in_ref[...]

    tile = x_ref.shape[0]
    x1 = x_ref[...] + _dot(yf_ref[...], wa_ref[...]) + _dot(o_ref[...], wb_ref[...])
    x1_ref[...] = x1
    h2 = _rms(x1, g_ref[...])
    logits = _dot(h2.astype(BF16), wr_ref[...])
    lane = lax.broadcasted_iota(I32, logits.shape, 1)
    neg = -jnp.inf

    def argmax_lowest(x, m):
        return jnp.min(jnp.where(x == m, lane, LANES), axis=1, keepdims=True)

    lg = jnp.where((lane >= N_EXPERTS) & (lane < N_EXPERTS + N_GROUPS), logits, neg)
    mg = jnp.max(lg, axis=1, keepdims=True)
    p_top = 1.0 / jnp.sum(jnp.exp(lg - mg), axis=1, keepdims=True)
    g_idx = argmax_lowest(lg, mg) - N_EXPERTS
    base = g_idx * EXPERTS_PER_GROUP
    le = jnp.where((lane >= base) & (lane < base + EXPERTS_PER_GROUP), logits, neg)
    v1 = jnp.max(le, axis=1, keepdims=True)
    i1 = argmax_lowest(le, v1)
    le2 = jnp.where(lane == i1, neg, le)
    v2 = jnp.max(le2, axis=1, keepdims=True)
    i2 = argmax_lowest(le2, v2)
    e2 = jnp.exp(v2 - v1)
    w1 = p_top / (1.0 + e2)
    w2 = w1 * e2

    a = jnp.minimum(i1, i2) - base
    b = jnp.maximum(i1, i2) - base
    pair = lax.shift_right_logical(a * (2 * EXPERTS_PER_GROUP - 1 - a), 1) + (b - a - 1)
    cls = g_idx * PAIRS_PER_GROUP + pair
    first_is_lo = i1 < i2
    w_lo = jnp.where(first_is_lo, w1, w2)
    w_hi = jnp.where(first_is_lo, w2, w1)

    onehot = jnp.where(lane == cls, 1.0, 0.0)
    before = _dot(tri_ref[...], onehot.astype(BF16))
    carry = cnt_ref[0:1, :]
    rank = jnp.sum(onehot * (carry + before), axis=1, keepdims=True)
    new_counts = jnp.broadcast_to(carry + jnp.sum(onehot, axis=0, keepdims=True), cnt_ref.shape)
    cnt_ref[...] = new_counts
    counts_ref[...] = new_counts

    slab = jnp.concatenate([h2, jnp.broadcast_to(w_lo, (tile, LANES)), jnp.broadcast_to(w_hi, (tile, LANES)),
                            jnp.zeros((tile, SLAB_WIDTH - D_MODEL - 2 * LANES), F32)], axis=1)
    slab_ref[...] = slab.reshape((tile,) + SLAB)

    meta = jnp.where(lane == META_CLASS, cls, jnp.where(lane == META_RANK, rank.astype(I32), 0))
    meta_ref[...] = meta.T[0:SUBLANES, :]


def _outproj(x2d, yf2d, o2d, wa, wb, g, wr, counts_in, tile):
    T = x2d.shape[0]
    tri = jnp.tril(jnp.ones((tile, tile), F32), -1).astype(BF16)
    tok = lambda w: pl.BlockSpec((tile, w), lambda i: (i, 0))
    full = lambda a: pl.BlockSpec(a.shape, lambda i: (0,) * a.ndim)
    return pl.pallas_call(
        _outproj_kernel,
        out_shape=(jax.ShapeDtypeStruct((T, D_MODEL), F32),
                   jax.ShapeDtypeStruct((T,) + SLAB, F32),
                   jax.ShapeDtypeStruct((SUBLANES, T), I32),
                   jax.ShapeDtypeStruct((SUBLANES, LANES), F32)),
        grid=(T // tile,),
        in_specs=[tok(D_MODEL), tok(FOURIER_WIDTH), tok(HGRN_WIDTH), full(wa), full(wb), full(g), full(wr),
                  full(tri), full(counts_in)],
        out_specs=(tok(D_MODEL), pl.BlockSpec((tile,) + SLAB, lambda i: (i,) + SLAB_ZEROS),
                   pl.BlockSpec((SUBLANES, tile), lambda i: (0, i)),
                   pl.BlockSpec((SUBLANES, LANES), lambda i: (0, 0))),
        scratch_shapes=[pltpu.VMEM((SUBLANES, LANES), F32)],
        compiler_params=_params("arbitrary"),
        name="outproj_router",
    )(x2d, yf2d, o2d, wa, wb, g, wr, tri, counts_in)


def _sorted_row(starts_ref, meta_ref, t):
    return starts_ref[meta_ref[META_CLASS, t]] + meta_ref[META_RANK, t]


def _start_row_copies(n, make):
    def body(t, carry):
        make(t).start()
        return carry
    lax.fori_loop(0, n, body, 0)


def _dispatch_kernel(starts_ref, fill_ref, *refs, bounds):
    n_trunks = len(bounds) - 1
    meta_refs, slab_refs = refs[0:2 * n_trunks:2], refs[1:2 * n_trunks:2]
    xs_hbm, zero_ref, sem = refs[2 * n_trunks:]
    i = pl.program_id(0)
    ftile = zero_ref.shape[0]

    @pl.when(i == 0)
    def _():
        zero_ref[...] = jnp.zeros_like(zero_ref)
        fill = lambda j: pltpu.make_async_copy(
            zero_ref, xs_hbm.at[pl.ds(pl.multiple_of(fill_ref[j] * ftile, ftile), ftile)], sem.at[1])

        def start(j, carry):
            @pl.when(fill_ref[j] >= 0)
            def _():
                fill(j).start()
            return carry

        def wait(j, carry):
            @pl.when(fill_ref[j] >= 0)
            def _():
                fill(j).wait()
            return carry

        lax.fori_loop(0, fill_ref.shape[0], start, 0)
        lax.fori_loop(0, fill_ref.shape[0], wait, 0)

    for k in range(n_trunks):
        @pl.when((i >= bounds[k]) & (i < bounds[k + 1]))
        def _(meta_ref=meta_refs[k], slab_ref=slab_refs[k]):
            tile = slab_ref.shape[0]
            _start_row_copies(tile, lambda t: pltpu.make_async_copy(
                slab_ref.at[t], xs_hbm.at[_sorted_row(starts_ref, meta_ref, t)], sem.at[0]))
            pltpu.make_async_copy(slab_ref, xs_hbm.at[pl.ds(0, tile)], sem.at[0]).wait()


def _dispatch(starts, fill_tiles, metas, slabs, n_rows, tile):
    steps = [slab.shape[0] // tile for slab in slabs]
    bounds = tuple(int(b) for b in np.cumsum([0] + steps))
    in_specs, args = [], []
    for k, (meta, slab) in enumerate(zip(metas, slabs)):
        local = lambda i, k=k: jnp.clip(i - bounds[k], 0, steps[k] - 1)
        in_specs.append(pl.BlockSpec((SUBLANES, tile), lambda i, s, f, local=local: (0, local(i)),
                                     memory_space=pltpu.SMEM))
        in_specs.append(pl.BlockSpec((tile,) + SLAB, lambda i, s, f, local=local: (local(i),) + SLAB_ZEROS))
        args += [meta, slab]
    return pl.pallas_call(
        functools.partial(_dispatch_kernel, bounds=bounds),
        out_shape=jax.ShapeDtypeStruct((n_rows,) + SLAB, F32),
        grid_spec=pltpu.PrefetchScalarGridSpec(
            num_scalar_prefetch=2, grid=(bounds[-1],),
            in_specs=in_specs,
            out_specs=pl.BlockSpec(memory_space=pl.ANY),
            scratch_shapes=[pltpu.VMEM((EXPERT_TILE,) + SLAB, F32), pltpu.SemaphoreType.DMA((2,))]),
        compiler_params=_params("arbitrary"),
        name="moe_dispatch",
    )(starts, fill_tiles, *args)


def _expert_kernel(elo_ref, ehi_ref, rows_ref, xs_ref, wg0_ref, wu0_ref, wd0_ref, wg1_ref, wu1_ref, wd1_ref,
                   ys_ref):
    i = pl.program_id(0)
    tile = xs_ref.shape[0]
    valid = rows_ref[i]

    @pl.when(valid > 0)
    def _():
        slab = xs_ref[...].reshape(tile, SLAB_WIDTH)
        x = slab[:, :D_MODEL].astype(BF16)
        y = None
        for k, (wg_ref, wu_ref, wd_ref) in enumerate(((wg0_ref, wu0_ref, wd0_ref), (wg1_ref, wu1_ref, wd1_ref))):
            a = _dot(x, wg_ref[0])
            hid = (a * jax.nn.sigmoid(a) * _dot(x, wu_ref[0])).astype(BF16)
            weight = jnp.tile(slab[:, D_MODEL + k * LANES:D_MODEL + (k + 1) * LANES], (1, D_MODEL // LANES))
            part = weight * _dot(hid, wd_ref[0])
            y = part if y is None else y + part
        ys_ref[...] = y.reshape(ys_ref.shape)

    @pl.when(valid == 0)
    def _():
        ys_ref[...] = jnp.zeros_like(ys_ref)


def _experts(tile_elo, tile_ehi, tile_rows, xs, wg, wu, wd, tile):
    n_tiles = xs.shape[0] // tile
    lo = lambda a: pl.BlockSpec((1,) + a.shape[1:], lambda i, elo, ehi, rows: (elo[i], 0, 0))
    hi = lambda a: pl.BlockSpec((1,) + a.shape[1:], lambda i, elo, ehi, rows: (ehi[i], 0, 0))
    return pl.pallas_call(
        _expert_kernel,
        out_shape=jax.ShapeDtypeStruct((xs.shape[0],) + ROW, F32),
        grid_spec=pltpu.PrefetchScalarGridSpec(
            num_scalar_prefetch=3, grid=(n_tiles,),
            in_specs=[pl.BlockSpec((tile,) + SLAB, lambda i, elo, ehi, rows: (i,) + SLAB_ZEROS),
                      lo(wg), lo(wu), lo(wd), hi(wg), hi(wu), hi(wd)],
            out_specs=pl.BlockSpec((tile,) + ROW, lambda i, elo, ehi, rows: (i, 0, 0))),
        compiler_params=_params("arbitrary"),
        name="moe_experts",
    )(tile_elo, tile_ehi, tile_rows, xs, wg, wu, wd, wg, wu, wd)


def _tail_kernel(starts_ref, meta_ref, meta_next_ref, ys_hbm, x1_ref, p_ref, gp_ref, wpg_ref, wpp_ref, gf_ref,
                 out_ref, buf_ref, sem, *, n):
    i = pl.program_id(0)
    tile = x1_ref.shape[0]

    def gather(m_ref, slot):
        _start_row_copies(tile, lambda t: pltpu.make_async_copy(
            ys_hbm.at[_sorted_row(starts_ref, m_ref, t)], buf_ref.at[slot, t], sem.at[slot]))

    @pl.when(i == 0)
    def _():
        gather(meta_ref, 0)

    @pl.when(i + 1 < n)
    def _():
        gather(meta_next_ref, (i + 1) % 2)

    ple = _dot(p_ref[...].astype(BF16), wpp_ref[...])
    slot = i % 2
    pltpu.make_async_copy(ys_hbm.at[pl.ds(0, tile)], buf_ref.at[slot], sem.at[slot]).wait()
    x2 = x1_ref[...] + buf_ref[slot].reshape(tile, D_MODEL)
    hp = _rms(x2, gp_ref[...]).astype(BF16)
    gate = jax.nn.sigmoid(_dot(hp, wpg_ref[...]))
    out_ref[...] = _rms(x2 + ple * gate, gf_ref[...])


def _tail(starts, meta, ys, x1, p2d, gp, wpg, wpp, gf, tile):
    T = x1.shape[0]
    n = T // tile
    tok = lambda w: pl.BlockSpec((tile, w), lambda i, s: (i, 0))
    full = lambda a: pl.BlockSpec(a.shape, lambda i, s: (0,) * a.ndim)
    return pl.pallas_call(
        functools.partial(_tail_kernel, n=n),
        out_shape=jax.ShapeDtypeStruct((T, D_MODEL), F32),
        grid_spec=pltpu.PrefetchScalarGridSpec(
            num_scalar_prefetch=1, grid=(n,),
            in_specs=[pl.BlockSpec((SUBLANES, tile), lambda i, s: (0, i), memory_space=pltpu.SMEM),
                      pl.BlockSpec((SUBLANES, tile), lambda i, s: (0, jnp.minimum(i + 1, n - 1)),
                                   memory_space=pltpu.SMEM),
                      pl.BlockSpec(memory_space=pl.ANY), tok(D_MODEL), tok(PLE_DIM),
                      full(gp), full(wpg), full(wpp), full(gf)],
            out_specs=tok(D_MODEL),
            scratch_shapes=[pltpu.VMEM((2, tile) + ROW, F32), pltpu.SemaphoreType.DMA((2,))]),
        compiler_params=_params("arbitrary"),
        name="combine_tail",
    )(starts, meta, meta, ys, x1, p2d, gp, wpg, wpp, gf)


def _class_experts():
    lo, hi = [], []
    for g in range(N_GROUPS):
        for a in range(EXPERTS_PER_GROUP):
            for b in range(a + 1, EXPERTS_PER_GROUP):
                lo.append(g * EXPERTS_PER_GROUP + a)
                hi.append(g * EXPERTS_PER_GROUP + b)
    return np.asarray(lo, np.int32), np.asarray(hi, np.int32)


def _sorted_layout(counts, n_tiles, tile):
    cnt = counts[0, :N_CLASSES].astype(I32)
    padded = (cnt + tile - 1) // tile * tile
    ends = jnp.cumsum(padded)
    starts = ends - padded
    tile_start = jnp.arange(n_tiles, dtype=I32) * tile
    tile_class = jnp.minimum(jnp.sum((ends[None, :] <= tile_start[:, None]).astype(I32), axis=1), N_CLASSES - 1)
    onehot = (tile_class[:, None] == jnp.arange(N_CLASSES, dtype=I32)[None, :]).astype(I32)
    pick = lambda table: jnp.sum(onehot * table[None, :], axis=1).astype(I32)
    tile_rows = jnp.clip(pick(cnt) - (tile_start - pick(starts)), 0, tile)
    tile_rows = jnp.where(tile_start < ends[-1], tile_rows, 0).astype(I32)
    class_lo, class_hi = _class_experts()
    partial = jnp.where(padded > cnt, ends // tile - 1, -1)
    tail = ends[-1] // tile + jnp.arange(N_CLASSES, dtype=I32)
    fill_tiles = jnp.concatenate([partial, jnp.where(tail < n_tiles, tail, -1)]).astype(I32)
    starts_padded = jnp.zeros((LANES,), I32).at[:N_CLASSES].set(starts)
    return starts_padded, pick(jnp.asarray(class_lo)), pick(jnp.asarray(class_hi)), tile_rows, fill_tiles


def _mix_and_route(x, w, counts_in):
    B, S, _ = x.shape
    T = B * S
    x2d = x.reshape(T, D_MODEL)
    z = _inproj(x2d, w["g_mix"], w["w_in"], tile=512)
    z3 = z.reshape(B, S, IN_WIDTH)
    yf = _fourier(z3, w["w_fourier"])
    o = _hgrn(z3, w["lb"], w["g_o"], heads=4 if S <= 2048 else 2)
    return _outproj(x2d, yf.reshape(T, FOURIER_WIDTH), o.reshape(T, HGRN_WIDTH),
                    w["w_out_a"], w["w_out_b"], w["g_ffn"], w["w_router"], counts_in, tile=ROUTE_TILE)


def kernel(x_prompt, x_sample, p_prompt, p_sample, norm_mix, w_in, w_fourier, lb_logits, norm_o, w_out,
           norm_ffn, w_route_group, w_route_expert, w_exp_gate, w_exp_up, w_exp_down, norm_ple,
           w_ple_gate, w_ple_proj, norm_final):
    assert w_in.shape[0] == 1, "single-layer trunk"
    lb_all = jnp.cumsum(jax.nn.softmax(lb_logits.astype(F32), axis=0), axis=0)
    router = jnp.concatenate(
        [w_route_expert[0], w_route_group[0],
         jnp.zeros((D_MODEL, LANES - N_EXPERTS - N_GROUPS), F32)], axis=1)
    w = {
        "g_mix": norm_mix[0][None, :],
        "w_in": w_in[0].astype(BF16),
        "w_fourier": w_fourier[0],
        "lb": lb_all[0],
        "g_o": norm_o[0][None, :],
        "w_out_a": w_out[0, :FOURIER_WIDTH].astype(BF16),
        "w_out_b": w_out[0, FOURIER_WIDTH:].astype(BF16),
        "g_ffn": norm_ffn[0][None, :],
        "w_router": router.astype(BF16),
    }
    xs_in = (x_prompt, x_sample)
    ps_in = (p_prompt[0], p_sample[0])

    counts = jnp.zeros((SUBLANES, LANES), F32)
    routed = []
    for x in xs_in:
        x1, slab, meta, counts = _mix_and_route(x, w, counts)
        routed.append((x1, slab, meta))

    total = sum(x.shape[0] * x.shape[1] for x in xs_in)
    n_tiles = total // EXPERT_TILE + N_CLASSES
    starts, tile_elo, tile_ehi, tile_rows, fill_tiles = _sorted_layout(counts, n_tiles, EXPERT_TILE)

    xs = _dispatch(starts, fill_tiles, [r[2] for r in routed], [r[1] for r in routed],
                   n_tiles * EXPERT_TILE, tile=ROUTE_TILE)
    ys = _experts(tile_elo, tile_ehi, tile_rows, xs, w_exp_gate[0].astype(BF16), w_exp_up[0].astype(BF16),
                  w_exp_down[0].astype(BF16), tile=EXPERT_TILE)

    outs = []
    for (x1, _, meta), x, p in zip(routed, xs_in, ps_in):
        out = _tail(starts, meta, ys, x1, p.reshape(-1, PLE_DIM), norm_ple[0][None, :],
                    w_ple_gate[0].astype(BF16), w_ple_proj[0].astype(BF16), norm_final[None, :],
                    tile=ROUTE_TILE)
        outs.append(out.reshape(x.shape))
    return tuple(outs)
```

```python
import functools
import math

import numpy as np
import jax
import jax.numpy as jnp
from jax import lax
from jax.experimental import pallas as pl
from jax.experimental.pallas import tpu as pltpu

F32 = jnp.float32
BF16 = jnp.bfloat16
I32 = jnp.int32

D_MODEL = 1024
FOURIER_WIDTH = 512
FOURIER_GROUPS = 4
GROUP_DIM = 128
HGRN_WIDTH = 512
HEAD_DIM = 128
HGRN_HEADS = 4
CHUNK = 64
IN_WIDTH = FOURIER_WIDTH + 5 * HGRN_WIDTH
N_GROUPS = 4
EXPERTS_PER_GROUP = 8
N_EXPERTS = 32
PAIRS_PER_GROUP = EXPERTS_PER_GROUP * (EXPERTS_PER_GROUP - 1) // 2
N_CLASSES = N_GROUPS * PAIRS_PER_GROUP
D_EXPERT = 512
PLE_DIM = 256
EPS = 1e-6
LANES = 128
SUBLANES = 8
VMEM_LIMIT = 56 * 1024 * 1024
ROUTE_TILE = 512
EXPERT_TILE = 256
ROW = (SUBLANES, LANES)
SLAB = ROW
SLAB_DTYPE = jnp.uint32
SLAB_WIDTH = D_MODEL
SLAB_ZEROS = (0,) * len(SLAB)
HALF = D_MODEL // 2
HIGH_HALF_MASK = 0xFFFF0000
META_CLASS, META_RANK = 0, 1

assert N_CLASSES <= LANES


def _params(*sem):
    return pltpu.CompilerParams(dimension_semantics=sem, vmem_limit_bytes=VMEM_LIMIT)


def _dot(a, b):
    return jnp.dot(a, b, preferred_element_type=F32)


def _dot_nt(a, b):
    return lax.dot_general(a, b, (((1,), (1,)), ((), ())), preferred_element_type=F32)


def _dot_tn(a, b):
    return lax.dot_general(a, b, (((0,), (0,)), ((), ())), preferred_element_type=F32)


def _rms(x, g):
    return x * lax.rsqrt(jnp.mean(x * x, axis=-1, keepdims=True) + EPS) * g


def _inproj_kernel(x_ref, g_ref, w_ref, z_ref):
    h = _rms(x_ref[...], g_ref[...]).astype(BF16)
    for j in range(0, IN_WIDTH, D_MODEL):
        z_ref[:, j:j + D_MODEL] = _dot(h, w_ref[:, j:j + D_MODEL]).astype(z_ref.dtype)


def _inproj(x2d, g, w_bf, tile):
    T = x2d.shape[0]
    return pl.pallas_call(
        _inproj_kernel,
        out_shape=jax.ShapeDtypeStruct((T, IN_WIDTH), BF16),
        grid=(T // tile,),
        in_specs=[
            pl.BlockSpec((tile, D_MODEL), lambda i: (i, 0)),
            pl.BlockSpec((1, D_MODEL), lambda i: (0, 0)),
            pl.BlockSpec((D_MODEL, IN_WIDTH), lambda i: (0, 0)),
        ],
        out_specs=pl.BlockSpec((tile, IN_WIDTH), lambda i: (i, 0)),
        compiler_params=_params("parallel"),
        name="inproj",
    )(x2d, g, w_bf)


def _fourier_fold_kernel(cc_ref, sc_ref, w_ref, m_ref, *, scale):
    for g in range(FOURIER_GROUPS):
        w = w_ref[g]
        m_ref[g, :GROUP_DIM, :] = (jnp.dot(cc_ref[...], w, preferred_element_type=F32,
                                           precision=lax.Precision.HIGHEST) * scale).astype(m_ref.dtype)
        m_ref[g, GROUP_DIM:, :] = (jnp.dot(sc_ref[...], w, preferred_element_type=F32,
                                           precision=lax.Precision.HIGHEST) * (-scale)).astype(m_ref.dtype)


def _fourier_fold(w_fourier, seq):
    cc, sc = _dft_tables(GROUP_DIM, GROUP_DIM, 1, 0, GROUP_DIM, F32)
    scale = 1.0 / math.sqrt(seq * GROUP_DIM)
    return pl.pallas_call(
        functools.partial(_fourier_fold_kernel, scale=scale),
        out_shape=jax.ShapeDtypeStruct((FOURIER_GROUPS, 2 * GROUP_DIM, GROUP_DIM), BF16),
        name="fourier_fold",
    )(cc, sc, w_fourier)


def _dft_tables(n_rows, n_cols, row_mult, row_off, period, dtype):
    r = int(round(math.sqrt(n_rows)))
    while n_rows % r:
        r -= 1
    m = jnp.arange(n_cols, dtype=I32)[None, :]
    k_lo = jnp.arange(r, dtype=I32)[:, None] * row_mult + row_off
    k_hi = jnp.arange(n_rows // r, dtype=I32)[:, None] * (r * row_mult)
    step = 2.0 * math.pi / period
    a_lo = (((k_lo % period) * m) % period).astype(F32) * step
    a_hi = (((k_hi % period) * m) % period).astype(F32) * step
    c_lo, s_lo = jnp.cos(a_lo)[None], jnp.sin(a_lo)[None]
    c_hi, s_hi = jnp.cos(a_hi)[:, None], jnp.sin(a_hi)[:, None]
    cos = (c_hi * c_lo - s_hi * s_lo).reshape(n_rows, n_cols)
    sin = (s_hi * c_lo + c_hi * s_lo).reshape(n_rows, n_cols)
    return cos.astype(dtype), sin.astype(dtype)


def _fourier_kernel(ulo_ref, uhi_ref, ce_ref, se_ref, co_ref, so_ref, m_ref, y_ref, sum_ref, dif_ref, stage_ref):
    tk = ce_ref.shape[0]

    @pl.when(pl.program_id(1) == 0)
    def _():
        lo = ulo_ref[0].astype(F32)
        hi = uhi_ref[0].astype(F32)
        sum_ref[...] = (lo + hi).astype(BF16)
        dif_ref[...] = (lo - hi).astype(BF16)

    def mix(cos_ref, sin_ref, x, parity):
        a = _dot(cos_ref[...], x).astype(BF16)
        b = _dot(sin_ref[...], x).astype(BF16)
        for g in range(FOURIER_GROUPS):
            sl = slice(g * GROUP_DIM, (g + 1) * GROUP_DIM)
            stage_ref[g, pl.ds(parity, tk, stride=2), :] = _dot(
                jnp.concatenate([a[:, sl], b[:, sl]], axis=1), m_ref[g])

    mix(ce_ref, se_ref, sum_ref[...], 0)
    mix(co_ref, so_ref, dif_ref[...], 1)
    for g in range(FOURIER_GROUPS):
        y_ref[0, :, g * GROUP_DIM:(g + 1) * GROUP_DIM] = stage_ref[g].astype(y_ref.dtype)


def _fourier(z3, w_fourier, tk=512):
    B, S, _ = z3.shape
    M = S // 2
    tk = min(tk, M)
    ce, se = _dft_tables(M, M, 1, 0, M, BF16)
    co, so = _dft_tables(M, M, 2, 1, S, BF16)
    m = _fourier_fold(w_fourier, S)
    table = pl.BlockSpec((tk, M), lambda b, k: (k, 0))
    return pl.pallas_call(
        _fourier_kernel,
        out_shape=jax.ShapeDtypeStruct((B, S, FOURIER_WIDTH), BF16),
        grid=(B, M // tk),
        in_specs=[
            pl.BlockSpec((1, M, FOURIER_WIDTH), lambda b, k: (b, 0, 0)),
            pl.BlockSpec((1, M, FOURIER_WIDTH), lambda b, k: (b, 1, 0)),
            table, table, table, table,
            pl.BlockSpec((FOURIER_GROUPS, 2 * GROUP_DIM, GROUP_DIM), lambda b, k: (0, 0, 0)),
        ],
        out_specs=pl.BlockSpec((1, 2 * tk, FOURIER_WIDTH), lambda b, k: (b, k, 0)),
        scratch_shapes=[pltpu.VMEM((M, FOURIER_WIDTH), BF16), pltpu.VMEM((M, FOURIER_WIDTH), BF16),
                        pltpu.VMEM((FOURIER_GROUPS, 2 * tk, GROUP_DIM), F32)],
        compiler_params=_params("parallel", "arbitrary"),
        name="fourier",
    )(z3, z3, ce, se, co, so, m)


def _split3(x):
    hi = x.astype(BF16)
    r1 = x - hi.astype(F32)
    mid = r1.astype(BF16)
    lo = (r1 - mid.astype(F32)).astype(BF16)
    return hi, mid, lo


def _hgrn_kernel(q_ref, v_ref, ff_ref, fb_ref, og_ref, lb_ref, go_ref, o_ref,
                 acc_ref, st_ref, qd_ref, ki_ref, kd_ref, dec_ref, *, seq, heads):
    n_chunks = seq // CHUNK
    half = n_chunks // 2
    assert half % 2 == 0
    row = lax.broadcasted_iota(I32, (CHUNK, CHUNK), 0)
    col = lax.broadcasted_iota(I32, (CHUNK, CHUNK), 1)
    masks = (row >= col, row <= col)
    tris = tuple(jnp.where(m, 1.0, 0.0).astype(BF16) for m in masks)
    edges = (CHUNK - 1, 0)
    f_refs = (ff_ref, fb_ref)
    inv_sqrt_d = HEAD_DIM ** -0.5

    st_ref[...] = jnp.zeros_like(st_ref)

    def chunk_rows(i, d):
        c = i if d == 0 else n_chunks - 1 - i
        return pl.ds(pl.multiple_of(c * CHUNK, CHUNK), CHUNK)

    def prepare(i, d, slot):
        rows = chunk_rows(jnp.minimum(i, n_chunks - 1), d)
        q = q_ref[0, rows, :].astype(F32)
        fr = f_refs[d][0, rows, :].astype(F32)
        lb = lb_ref[d:d + 1, :]
        f = lb + (1.0 - lb) * jax.nn.sigmoid(fr)
        k = 1.0 - f
        hi, mid, lo = _split3(jnp.log(f))
        b = _dot(tris[d], hi) + _dot(tris[d], mid) + _dot(tris[d], lo)
        b_edge = b[edges[d]:edges[d] + 1, :]
        qd_ref[d, slot] = (q * jnp.exp(b) * inv_sqrt_d).astype(BF16)
        ki_ref[d, slot] = (k * jnp.exp(-b)).astype(BF16)
        kd_ref[d, slot] = (k * jnp.exp(b_edge - b)).astype(BF16)
        dec_ref[d, slot] = jnp.broadcast_to(jnp.exp(b_edge), dec_ref.shape[2:])

    def advance(i, d, slot):
        rows = chunk_rows(i, d)
        v = v_ref[0, rows, :]
        q_dec, k_inv, k_dec = qd_ref[d, slot], ki_ref[d, slot], kd_ref[d, slot]
        dec = dec_ref[d, slot, 0:1, :]
        outs = []
        for h in range(heads):
            sl = slice(h * HEAD_DIM, (h + 1) * HEAD_DIM)
            st = st_ref[d, h]
            scores = jnp.where(masks[d], _dot_nt(q_dec[:, sl], k_inv[:, sl]), 0.0).astype(BF16)
            outs.append(_dot(scores, v[:, sl]) + _dot_nt(q_dec[:, sl], st.astype(BF16)))
            st_ref[d, h] = st * dec[:, sl] + _dot_tn(v[:, sl], k_dec[:, sl])
        return rows, outs

    def first_touch(rows, outs):
        for h in range(heads):
            acc_ref[rows, h * HEAD_DIM:(h + 1) * HEAD_DIM] = outs[h]

    def finish(rows, outs):
        og = og_ref[0, rows, :].astype(F32)
        for h in range(heads):
            sl = slice(h * HEAD_DIM, (h + 1) * HEAD_DIM)
            o = outs[h] + acc_ref[rows, sl]
            gate = og[:, sl] * jax.nn.sigmoid(og[:, sl])
            o_ref[0, rows, sl] = (_rms(o, go_ref[...]) * gate).astype(o_ref.dtype)

    def two_steps(sink):
        def body(j, carry):
            i = 2 * j
            for slot in (0, 1):
                for d in (0, 1):
                    prepare(i + slot + 1, d, 1 - slot)
                for d in (0, 1):
                    sink(*advance(i + slot, d, slot))
            return carry
        return body

    for d in (0, 1):
        prepare(0, d, 0)
    lax.fori_loop(0, half // 2, two_steps(first_touch), 0)
    lax.fori_loop(half // 2, n_chunks // 2, two_steps(finish), 0)


def _hgrn(z3, lb, g_o, heads):
    B, S, _ = z3.shape
    assert (S // CHUNK) % 2 == 0 and HGRN_HEADS % heads == 0
    W = heads * HEAD_DIM
    nblk = HGRN_WIDTH // W
    base = FOURIER_WIDTH // W

    def zspec(j):
        return pl.BlockSpec((1, S, W), lambda b, h, j=j: (b, 0, base + j * nblk + h))

    return pl.pallas_call(
        functools.partial(_hgrn_kernel, seq=S, heads=heads),
        out_shape=jax.ShapeDtypeStruct((B, S, HGRN_WIDTH), BF16),
        grid=(B, nblk),
        in_specs=[zspec(0), zspec(1), zspec(2), zspec(3), zspec(4),
                  pl.BlockSpec((2, W), lambda b, h: (0, h)),
                  pl.BlockSpec((1, HEAD_DIM), lambda b, h: (0, 0))],
        out_specs=pl.BlockSpec((1, S, W), lambda b, h: (b, 0, h)),
        scratch_shapes=[pltpu.VMEM((S, W), F32),
                        pltpu.VMEM((2, heads, HEAD_DIM, HEAD_DIM), F32),
                        pltpu.VMEM((2, 2, CHUNK, W), BF16),
                        pltpu.VMEM((2, 2, CHUNK, W), BF16),
                        pltpu.VMEM((2, 2, CHUNK, W), BF16),
                        pltpu.VMEM((2, 2, SUBLANES, W), F32)],
        compiler_params=_params("parallel", "arbitrary"),
        name="hgrn",
    )(z3, z3, z3, z3, z3, lb, g_o)


def _outproj_kernel(x_ref, yf_ref, o_ref, wa_ref, wb_ref, g_ref, wr_ref, tri_ref, cnt_in_ref,
                    x1_ref, slab_ref, meta_ref, counts_ref, cnt_ref):
    @pl.when(pl.program_id(0) == 0)
    def _():
        cnt_ref[...] = cnt_in_ref[...]

    tile = x_ref.shape[0]
    x1 = x_ref[...] + _dot(yf_ref[...], wa_ref[...]) + _dot(o_ref[...], wb_ref[...])
    x1_ref[...] = x1
    hb = _rms(x1, g_ref[...]).astype(BF16)
    logits = _dot(hb, wr_ref[...])
    lane = lax.broadcasted_iota(I32, logits.shape, 1)
    neg = -jnp.inf

    def argmax_lowest(x, m):
        return jnp.min(jnp.where(x == m, lane, LANES), axis=1, keepdims=True)

    lg = jnp.where((lane >= N_EXPERTS) & (lane < N_EXPERTS + N_GROUPS), logits, neg)
    mg = jnp.max(lg, axis=1, keepdims=True)
    p_top = 1.0 / jnp.sum(jnp.exp(lg - mg), axis=1, keepdims=True)
    g_idx = argmax_lowest(lg, mg) - N_EXPERTS
    base = g_idx * EXPERTS_PER_GROUP
    le = jnp.where((lane >= base) & (lane < base + EXPERTS_PER_GROUP), logits, neg)
    v1 = jnp.max(le, axis=1, keepdims=True)
    i1 = argmax_lowest(le, v1)
    le2 = jnp.where(lane == i1, neg, le)
    v2 = jnp.max(le2, axis=1, keepdims=True)
    i2 = argmax_lowest(le2, v2)
    e2 = jnp.exp(v2 - v1)
    w1 = p_top / (1.0 + e2)
    w2 = w1 * e2

    a = jnp.minimum(i1, i2) - base
    b = jnp.maximum(i1, i2) - base
    pair = lax.shift_right_logical(a * (2 * EXPERTS_PER_GROUP - 1 - a), 1) + (b - a - 1)
    cls = g_idx * PAIRS_PER_GROUP + pair
    first_is_lo = i1 < i2
    w_lo = jnp.where(first_is_lo, w1, w2)
    w_hi = jnp.where(first_is_lo, w2, w1)

    onehot = jnp.where(lane == cls, 1.0, 0.0)
    before = _dot(tri_ref[...], onehot.astype(BF16))
    carry = cnt_ref[0:1, :]
    rank = jnp.sum(onehot * (carry + before), axis=1, keepdims=True)
    new_counts = jnp.broadcast_to(carry + jnp.sum(onehot, axis=0, keepdims=True), cnt_ref.shape)
    cnt_ref[...] = new_counts
    counts_ref[...] = new_counts

    bits = lambda v: lax.bitcast_convert_type(v, SLAB_DTYPE)
    hb32 = hb.astype(F32)
    packed = (lax.shift_right_logical(bits(hb32[:, :HALF]), jnp.uint32(16))
              | (bits(hb32[:, HALF:]) & jnp.uint32(HIGH_HALF_MASK)))
    slab = jnp.concatenate([packed, bits(jnp.broadcast_to(w_lo, (tile, LANES))),
                            bits(jnp.broadcast_to(w_hi, (tile, LANES))),
                            jnp.zeros((tile, SLAB_WIDTH - HALF - 2 * LANES), SLAB_DTYPE)], axis=1)
    slab_ref[...] = slab.reshape((tile,) + SLAB)

    meta = jnp.where(lane == META_CLASS, cls, jnp.where(lane == META_RANK, rank.astype(I32), 0))
    meta_ref[...] = meta.T[0:SUBLANES, :]


def _outproj(x2d, yf2d, o2d, wa, wb, g, wr, counts_in, tile):
    T = x2d.shape[0]
    tri = jnp.tril(jnp.ones((tile, tile), F32), -1).astype(BF16)
    tok = lambda w: pl.BlockSpec((tile, w), lambda i: (i, 0))
    full = lambda a: pl.BlockSpec(a.shape, lambda i: (0,) * a.ndim)
    return pl.pallas_call(
        _outproj_kernel,
        out_shape=(jax.ShapeDtypeStruct((T, D_MODEL), F32),
                   jax.ShapeDtypeStruct((T,) + SLAB, SLAB_DTYPE),
                   jax.ShapeDtypeStruct((SUBLANES, T), I32),
                   jax.ShapeDtypeStruct((SUBLANES, LANES), F32)),
        grid=(T // tile,),
        in_specs=[tok(D_MODEL), tok(FOURIER_WIDTH), tok(HGRN_WIDTH), full(wa), full(wb), full(g), full(wr),
                  full(tri), full(counts_in)],
        out_specs=(tok(D_MODEL), pl.BlockSpec((tile,) + SLAB, lambda i: (i,) + SLAB_ZEROS),
                   pl.BlockSpec((SUBLANES, tile), lambda i: (0, i)),
                   pl.BlockSpec((SUBLANES, LANES), lambda i: (0, 0))),
        scratch_shapes=[pltpu.VMEM((SUBLANES, LANES), F32)],
        compiler_params=_params("arbitrary"),
        name="outproj_router",
    )(x2d, yf2d, o2d, wa, wb, g, wr, tri, counts_in)


def _sorted_row(starts_ref, meta_ref, t):
    return starts_ref[meta_ref[META_CLASS, t]] + meta_ref[META_RANK, t]


ROW_COPY_UNROLL = 8


def _start_row_copies(n, make):
    assert n % ROW_COPY_UNROLL == 0

    def body(j, carry):
        for u in range(ROW_COPY_UNROLL):
            make(j * ROW_COPY_UNROLL + u).start()
        return carry
    lax.fori_loop(0, n // ROW_COPY_UNROLL, body, 0)


def _dispatch_kernel(starts_ref, fill_ref, *refs, bounds):
    n_trunks = len(bounds) - 1
    meta_refs, slab_refs = refs[0:2 * n_trunks:2], refs[1:2 * n_trunks:2]
    xs_hbm, zero_ref, sem = refs[2 * n_trunks:]
    i = pl.program_id(0)
    ftile = zero_ref.shape[0]

    @pl.when(i == 0)
    def _():
        zero_ref[...] = jnp.zeros_like(zero_ref)
        fill = lambda j: pltpu.make_async_copy(
            zero_ref, xs_hbm.at[pl.ds(pl.multiple_of(fill_ref[j] * ftile, ftile), ftile)], sem.at[1])

        def start(j, carry):
            @pl.when(fill_ref[j] >= 0)
            def _():
                fill(j).start()
            return carry

        def wait(j, carry):
            @pl.when(fill_ref[j] >= 0)
            def _():
                fill(j).wait()
            return carry

        lax.fori_loop(0, fill_ref.shape[0], start, 0)
        lax.fori_loop(0, fill_ref.shape[0], wait, 0)

    for k in range(n_trunks):
        @pl.when((i >= bounds[k]) & (i < bounds[k + 1]))
        def _(meta_ref=meta_refs[k], slab_ref=slab_refs[k]):
            tile = slab_ref.shape[0]
            _start_row_copies(tile, lambda t: pltpu.make_async_copy(
                slab_ref.at[t], xs_hbm.at[_sorted_row(starts_ref, meta_ref, t)], sem.at[0]))
            pltpu.make_async_copy(slab_ref, xs_hbm.at[pl.ds(0, tile)], sem.at[0]).wait()


def _dispatch(starts, fill_tiles, metas, slabs, n_rows, tile):
    steps = [slab.shape[0] // tile for slab in slabs]
    bounds = tuple(int(b) for b in np.cumsum([0] + steps))
    in_specs, args = [], []
    for k, (meta, slab) in enumerate(zip(metas, slabs)):
        local = lambda i, k=k: jnp.clip(i - bounds[k], 0, steps[k] - 1)
        in_specs.append(pl.BlockSpec((SUBLANES, tile), lambda i, s, f, local=local: (0, local(i)),
                                     memory_space=pltpu.SMEM))
        in_specs.append(pl.BlockSpec((tile,) + SLAB, lambda i, s, f, local=local: (local(i),) + SLAB_ZEROS))
        args += [meta, slab]
    return pl.pallas_call(
        functools.partial(_dispatch_kernel, bounds=bounds),
        out_shape=jax.ShapeDtypeStruct((n_rows,) + SLAB, SLAB_DTYPE),
        grid_spec=pltpu.PrefetchScalarGridSpec(
            num_scalar_prefetch=2, grid=(bounds[-1],),
            in_specs=in_specs,
            out_specs=pl.BlockSpec(memory_space=pl.ANY),
            scratch_shapes=[pltpu.VMEM((EXPERT_TILE,) + SLAB, SLAB_DTYPE), pltpu.SemaphoreType.DMA((2,))]),
        compiler_params=_params("arbitrary"),
        name="moe_dispatch",
    )(starts, fill_tiles, *args)


def _expert_kernel(elo_ref, ehi_ref, rows_ref, xs_ref, wg0_ref, wu0_ref, wd0_ref, wg1_ref, wu1_ref, wd1_ref,
                   ys_ref):
    i = pl.program_id(0)
    tile = xs_ref.shape[0]
    valid = rows_ref[i]

    @pl.when(valid > 0)
    def _():
        slab = xs_ref[...].reshape(tile, SLAB_WIDTH)
        as_f32 = lambda v: lax.bitcast_convert_type(v, F32)
        packed = slab[:, :HALF]
        x = jnp.concatenate([as_f32(lax.shift_left(packed, jnp.uint32(16))),
                             as_f32(packed & jnp.uint32(HIGH_HALF_MASK))], axis=1).astype(BF16)
        y = None
        for k, (wg_ref, wu_ref, wd_ref) in enumerate(((wg0_ref, wu0_ref, wd0_ref), (wg1_ref, wu1_ref, wd1_ref))):
            a = _dot(x, wg_ref[0])
            hid = (a * jax.nn.sigmoid(a) * _dot(x, wu_ref[0])).astype(BF16)
            weight = jnp.tile(as_f32(slab[:, HALF + k * LANES:HALF + (k + 1) * LANES]), (1, D_MODEL // LANES))
            part = weight * _dot(hid, wd_ref[0])
            y = part if y is None else y + part
        ys_ref[...] = y.reshape(ys_ref.shape)

    @pl.when(valid == 0)
    def _():
        ys_ref[...] = jnp.zeros_like(ys_ref)


def _experts(tile_elo, tile_ehi, tile_rows, xs, wg, wu, wd, tile):
    n_tiles = xs.shape[0] // tile
    lo = lambda a: pl.BlockSpec((1,) + a.shape[1:], lambda i, elo, ehi, rows: (elo[i], 0, 0))
    hi = lambda a: pl.BlockSpec((1,) + a.shape[1:], lambda i, elo, ehi, rows: (ehi[i], 0, 0))
    return pl.pallas_call(
        _expert_kernel,
        out_shape=jax.ShapeDtypeStruct((xs.shape[0],) + ROW, F32),
        grid_spec=pltpu.PrefetchScalarGridSpec(
            num_scalar_prefetch=3, grid=(n_tiles,),
            in_specs=[pl.BlockSpec((tile,) + SLAB, lambda i, elo, ehi, rows: (i,) + SLAB_ZEROS),
                      lo(wg), lo(wu), lo(wd), hi(wg), hi(wu), hi(wd)],
            out_specs=pl.BlockSpec((tile,) + ROW, lambda i, elo, ehi, rows: (i, 0, 0))),
        compiler_params=_params("arbitrary"),
        name="moe_experts",
    )(tile_elo, tile_ehi, tile_rows, xs, wg, wu, wd, wg, wu, wd)


def _tail_kernel(starts_ref, meta_ref, meta_next_ref, ys_hbm, x1_ref, p_ref, gp_ref, wpg_ref, wpp_ref, gf_ref,
                 out_ref, buf_ref, sem, *, n):
    i = pl.program_id(0)
    tile = x1_ref.shape[0]

    def gather(m_ref, slot):
        _start_row_copies(tile, lambda t: pltpu.make_async_copy(
            ys_hbm.at[_sorted_row(starts_ref, m_ref, t)], buf_ref.at[slot, t], sem.at[slot]))

    @pl.when(i == 0)
    def _():
        gather(meta_ref, 0)

    @pl.when(i + 1 < n)
    def _():
        gather(meta_next_ref, (i + 1) % 2)

    ple = _dot(p_ref[...].astype(BF16), wpp_ref[...])
    slot = i % 2
    pltpu.make_async_copy(ys_hbm.at[pl.ds(0, tile)], buf_ref.at[slot], sem.at[slot]).wait()
    x2 = x1_ref[...] + buf_ref[slot].reshape(tile, D_MODEL)
    hp = _rms(x2, gp_ref[...]).astype(BF16)
    gate = jax.nn.sigmoid(_dot(hp, wpg_ref[...]))
    out_ref[...] = _rms(x2 + ple * gate, gf_ref[...])


def _tail(starts, meta, ys, x1, p2d, gp, wpg, wpp, gf, tile):
    T = x1.shape[0]
    n = T // tile
    tok = lambda w: pl.BlockSpec((tile, w), lambda i, s: (i, 0))
    full = lambda a: pl.BlockSpec(a.shape, lambda i, s: (0,) * a.ndim)
    return pl.pallas_call(
        functools.partial(_tail_kernel, n=n),
        out_shape=jax.ShapeDtypeStruct((T, D_MODEL), F32),
        grid_spec=pltpu.PrefetchScalarGridSpec(
            num_scalar_prefetch=1, grid=(n,),
            in_specs=[pl.BlockSpec((SUBLANES, tile), lambda i, s: (0, i), memory_space=pltpu.SMEM),
                      pl.BlockSpec((SUBLANES, tile), lambda i, s: (0, jnp.minimum(i + 1, n - 1)),
                                   memory_space=pltpu.SMEM),
                      pl.BlockSpec(memory_space=pl.ANY), tok(D_MODEL), tok(PLE_DIM),
                      full(gp), full(wpg), full(wpp), full(gf)],
            out_specs=tok(D_MODEL),
            scratch_shapes=[pltpu.VMEM((2, tile) + ROW, F32), pltpu.SemaphoreType.DMA((2,))]),
        compiler_params=_params("arbitrary"),
        name="combine_tail",
    )(starts, meta, meta, ys, x1, p2d, gp, wpg, wpp, gf)


def _class_experts():
    lo, hi = [], []
    for g in range(N_GROUPS):
        for a in range(EXPERTS_PER_GROUP):
            for b in range(a + 1, EXPERTS_PER_GROUP):
                lo.append(g * EXPERTS_PER_GROUP + a)
                hi.append(g * EXPERTS_PER_GROUP + b)
    return np.asarray(lo, np.int32), np.asarray(hi, np.int32)


def _sorted_layout(counts, n_tiles, tile):
    cnt = counts[0, :N_CLASSES].astype(I32)
    padded = (cnt + tile - 1) // tile * tile
    ends = jnp.cumsum(padded)
    starts = ends - padded
    tile_start = jnp.arange(n_tiles, dtype=I32) * tile
    tile_class = jnp.minimum(jnp.sum((ends[None, :] <= tile_start[:, None]).astype(I32), axis=1), N_CLASSES - 1)
    onehot = (tile_class[:, None] == jnp.arange(N_CLASSES, dtype=I32)[None, :]).astype(I32)
    pick = lambda table: jnp.sum(onehot * table[None, :], axis=1).astype(I32)
    tile_rows = jnp.clip(pick(cnt) - (tile_start - pick(starts)), 0, tile)
    tile_rows = jnp.where(tile_start < ends[-1], tile_rows, 0).astype(I32)
    class_lo, class_hi = _class_experts()
    partial = jnp.where(padded > cnt, ends // tile - 1, -1)
    tail = ends[-1] // tile + jnp.arange(N_CLASSES, dtype=I32)
    fill_tiles = jnp.concatenate([partial, jnp.where(tail < n_tiles, tail, -1)]).astype(I32)
    starts_padded = jnp.zeros((LANES,), I32).at[:N_CLASSES].set(starts)
    return starts_padded, pick(jnp.asarray(class_lo)), pick(jnp.asarray(class_hi)), tile_rows, fill_tiles


def _mix_and_route(x, w, counts_in):
    B, S, _ = x.shape
    T = B * S
    x2d = x.reshape(T, D_MODEL)
    z = _inproj(x2d, w["g_mix"], w["w_in"], tile=512)
    z3 = z.reshape(B, S, IN_WIDTH)
    yf = _fourier(z3, w["w_fourier"])
    o = _hgrn(z3, w["lb"], w["g_o"], heads=4 if S <= 2048 else 2)
    return _outproj(x2d, yf.reshape(T, FOURIER_WIDTH), o.reshape(T, HGRN_WIDTH),
                    w["w_out_a"], w["w_out_b"], w["g_ffn"], w["w_router"], counts_in, tile=ROUTE_TILE)


def kernel(x_prompt, x_sample, p_prompt, p_sample, norm_mix, w_in, w_fourier, lb_logits, norm_o, w_out,
           norm_ffn, w_route_group, w_route_expert, w_exp_gate, w_exp_up, w_exp_down, norm_ple,
           w_ple_gate, w_ple_proj, norm_final):
    assert w_in.shape[0] == 1, "single-layer trunk"
    lb_all = jnp.cumsum(jax.nn.softmax(lb_logits.astype(F32), axis=0), axis=0)
    router = jnp.concatenate(
        [w_route_expert[0], w_route_group[0],
         jnp.zeros((D_MODEL, LANES - N_EXPERTS - N_GROUPS), F32)], axis=1)
    w = {
        "g_mix": norm_mix[0][None, :],
        "w_in": w_in[0].astype(BF16),
        "w_fourier": w_fourier[0],
        "lb": lb_all[0],
        "g_o": norm_o[0][None, :],
        "w_out_a": w_out[0, :FOURIER_WIDTH].astype(BF16),
        "w_out_b": w_out[0, FOURIER_WIDTH:].astype(BF16),
        "g_ffn": norm_ffn[0][None, :],
        "w_router": router.astype(BF16),
    }
    xs_in = (x_prompt, x_sample)
    ps_in = (p_prompt[0], p_sample[0])

    counts = jnp.zeros((SUBLANES, LANES), F32)
    routed = []
    for x in xs_in:
        x1, slab, meta, counts = _mix_and_route(x, w, counts)
        routed.append((x1, slab, meta))

    total = sum(x.shape[0] * x.shape[1] for x in xs_in)
    n_tiles = total // EXPERT_TILE + N_CLASSES
    starts, tile_elo, tile_ehi, tile_rows, fill_tiles = _sorted_layout(counts, n_tiles, EXPERT_TILE)

    xs = _dispatch(starts, fill_tiles, [r[2] for r in routed], [r[1] for r in routed],
                   n_tiles * EXPERT_TILE, tile=ROUTE_TILE)
    ys = _experts(tile_elo, tile_ehi, tile_rows, xs, w_exp_gate[0].astype(BF16), w_exp_up[0].astype(BF16),
                  w_exp_down[0].astype(BF16), tile=EXPERT_TILE)

    outs = []
    for (x1, _, meta), x, p in zip(routed, xs_in, ps_in):
        out = _tail(starts, meta, ys, x1, p.reshape(-1, PLE_DIM), norm_ple[0][None, :],
                    w_ple_gate[0].astype(BF16), w_ple_proj[0].astype(BF16), norm_final[None, :],
                    tile=ROUTE_TILE)
        outs.append(out.reshape(x.shape))
    return tuple(outs)
```

```python
import functools
import math

import numpy as np
import jax
import jax.numpy as jnp
from jax import lax
from jax.experimental import pallas as pl
from jax.experimental.pallas import tpu as pltpu

F32 = jnp.float32
BF16 = jnp.bfloat16
I32 = jnp.int32

D_MODEL = 1024
FOURIER_WIDTH = 512
FOURIER_GROUPS = 4
GROUP_DIM = 128
HGRN_WIDTH = 512
HEAD_DIM = 128
HGRN_HEADS = 4
CHUNK = 64
IN_WIDTH = FOURIER_WIDTH + 5 * HGRN_WIDTH
N_GROUPS = 4
EXPERTS_PER_GROUP = 8
N_EXPERTS = 32
PAIRS_PER_GROUP = EXPERTS_PER_GROUP * (EXPERTS_PER_GROUP - 1) // 2
N_CLASSES = N_GROUPS * PAIRS_PER_GROUP
D_EXPERT = 512
PLE_DIM = 256
EPS = 1e-6
LANES = 128
SUBLANES = 8
VMEM_LIMIT = 56 * 1024 * 1024
ROUTE_TILE = 512
EXPERT_TILE = 256
ROW = (SUBLANES, LANES)
SLAB = (2 * SUBLANES, LANES)
SLAB_DTYPE = BF16
SLAB_WIDTH = 2 * D_MODEL
SLAB_ZEROS = (0,) * len(SLAB)
META_CLASS, META_RANK = 0, 1

assert N_CLASSES <= LANES


def _params(*sem):
    return pltpu.CompilerParams(dimension_semantics=sem, vmem_limit_bytes=VMEM_LIMIT)


def _dot(a, b):
    return jnp.dot(a, b, preferred_element_type=F32)


def _dot_nt(a, b):
    return lax.dot_general(a, b, (((1,), (1,)), ((), ())), preferred_element_type=F32)


def _dot_tn(a, b):
    return lax.dot_general(a, b, (((0,), (0,)), ((), ())), preferred_element_type=F32)


def _rms(x, g):
    return x * lax.rsqrt(jnp.mean(x * x, axis=-1, keepdims=True) + EPS) * g


def _inproj_kernel(x_ref, g_ref, w_ref, z_ref):
    h = _rms(x_ref[...], g_ref[...]).astype(BF16)
    for j in range(0, IN_WIDTH, D_MODEL):
        z_ref[:, j:j + D_MODEL] = _dot(h, w_ref[:, j:j + D_MODEL]).astype(z_ref.dtype)


def _inproj(x2d, g, w_bf, tile):
    T = x2d.shape[0]
    return pl.pallas_call(
        _inproj_kernel,
        out_shape=jax.ShapeDtypeStruct((T, IN_WIDTH), BF16),
        grid=(T // tile,),
        in_specs=[
            pl.BlockSpec((tile, D_MODEL), lambda i: (i, 0)),
            pl.BlockSpec((1, D_MODEL), lambda i: (0, 0)),
            pl.BlockSpec((D_MODEL, IN_WIDTH), lambda i: (0, 0)),
        ],
        out_specs=pl.BlockSpec((tile, IN_WIDTH), lambda i: (i, 0)),
        compiler_params=_params("parallel"),
        name="inproj",
    )(x2d, g, w_bf)


def _fourier_fold_kernel(cc_ref, sc_ref, w_ref, m_ref, *, scale):
    for g in range(FOURIER_GROUPS):
        w = w_ref[g]
        m_ref[g, :GROUP_DIM, :] = (jnp.dot(cc_ref[...], w, preferred_element_type=F32,
                                           precision=lax.Precision.HIGHEST) * scale).astype(m_ref.dtype)
        m_ref[g, GROUP_DIM:, :] = (jnp.dot(sc_ref[...], w, preferred_element_type=F32,
                                           precision=lax.Precision.HIGHEST) * (-scale)).astype(m_ref.dtype)


def _fourier_fold(w_fourier, seq):
    cc, sc = _dft_tables(GROUP_DIM, GROUP_DIM, 1, 0, GROUP_DIM, F32)
    scale = 1.0 / math.sqrt(seq * GROUP_DIM)
    return pl.pallas_call(
        functools.partial(_fourier_fold_kernel, scale=scale),
        out_shape=jax.ShapeDtypeStruct((FOURIER_GROUPS, 2 * GROUP_DIM, GROUP_DIM), BF16),
        name="fourier_fold",
    )(cc, sc, w_fourier)


def _dft_tables(n_rows, n_cols, row_mult, row_off, period, dtype):
    r = int(round(math.sqrt(n_rows)))
    while n_rows % r:
        r -= 1
    m = jnp.arange(n_cols, dtype=I32)[None, :]
    k_lo = jnp.arange(r, dtype=I32)[:, None] * row_mult + row_off
    k_hi = jnp.arange(n_rows // r, dtype=I32)[:, None] * (r * row_mult)
    step = 2.0 * math.pi / period
    a_lo = (((k_lo % period) * m) % period).astype(F32) * step
    a_hi = (((k_hi % period) * m) % period).astype(F32) * step
    c_lo, s_lo = jnp.cos(a_lo)[None], jnp.sin(a_lo)[None]
    c_hi, s_hi = jnp.cos(a_hi)[:, None], jnp.sin(a_hi)[:, None]
    cos = (c_hi * c_lo - s_hi * s_lo).reshape(n_rows, n_cols)
    sin = (s_hi * c_lo + c_hi * s_lo).reshape(n_rows, n_cols)
    return cos.astype(dtype), sin.astype(dtype)


def _fourier_kernel(ulo_ref, uhi_ref, ce_ref, se_ref, co_ref, so_ref, m_ref, y_ref, sum_ref, dif_ref, stage_ref):
    tk = ce_ref.shape[0]

    @pl.when(pl.program_id(1) == 0)
    def _():
        lo = ulo_ref[0].astype(F32)
        hi = uhi_ref[0].astype(F32)
        sum_ref[...] = (lo + hi).astype(BF16)
        dif_ref[...] = (lo - hi).astype(BF16)

    def mix(cos_ref, sin_ref, x, parity):
        a = _dot(cos_ref[...], x).astype(BF16)
        b = _dot(sin_ref[...], x).astype(BF16)
        for g in range(FOURIER_GROUPS):
            sl = slice(g * GROUP_DIM, (g + 1) * GROUP_DIM)
            stage_ref[g, pl.ds(parity, tk, stride=2), :] = _dot(
                jnp.concatenate([a[:, sl], b[:, sl]], axis=1), m_ref[g])

    mix(ce_ref, se_ref, sum_ref[...], 0)
    mix(co_ref, so_ref, dif_ref[...], 1)
    for g in range(FOURIER_GROUPS):
        y_ref[0, :, g * GROUP_DIM:(g + 1) * GROUP_DIM] = stage_ref[g].astype(y_ref.dtype)


def _fourier(z3, w_fourier, tk=512):
    B, S, _ = z3.shape
    M = S // 2
    tk = min(tk, M)
    ce, se = _dft_tables(M, M, 1, 0, M, BF16)
    co, so = _dft_tables(M, M, 2, 1, S, BF16)
    m = _fourier_fold(w_fourier, S)
    table = pl.BlockSpec((tk, M), lambda b, k: (k, 0))
    return pl.pallas_call(
        _fourier_kernel,
        out_shape=jax.ShapeDtypeStruct((B, S, FOURIER_WIDTH), BF16),
        grid=(B, M // tk),
        in_specs=[
            pl.BlockSpec((1, M, FOURIER_WIDTH), lambda b, k: (b, 0, 0)),
            pl.BlockSpec((1, M, FOURIER_WIDTH), lambda b, k: (b, 1, 0)),
            table, table, table, table,
            pl.BlockSpec((FOURIER_GROUPS, 2 * GROUP_DIM, GROUP_DIM), lambda b, k: (0, 0, 0)),
        ],
        out_specs=pl.BlockSpec((1, 2 * tk, FOURIER_WIDTH), lambda b, k: (b, k, 0)),
        scratch_shapes=[pltpu.VMEM((M, FOURIER_WIDTH), BF16), pltpu.VMEM((M, FOURIER_WIDTH), BF16),
                        pltpu.VMEM((FOURIER_GROUPS, 2 * tk, GROUP_DIM), F32)],
        compiler_params=_params("parallel", "arbitrary"),
        name="fourier",
    )(z3, z3, ce, se, co, so, m)


def _split3(x):
    hi = x.astype(BF16)
    r1 = x - hi.astype(F32)
    mid = r1.astype(BF16)
    lo = (r1 - mid.astype(F32)).astype(BF16)
    return hi, mid, lo


def _hgrn_kernel(q_ref, v_ref, ff_ref, fb_ref, og_ref, lb_ref, go_ref, o_ref,
                 acc_ref, st_ref, qd_ref, ki_ref, kd_ref, dec_ref, *, seq, heads):
    n_chunks = seq // CHUNK
    half = n_chunks // 2
    assert half % 2 == 0
    row = lax.broadcasted_iota(I32, (CHUNK, CHUNK), 0)
    col = lax.broadcasted_iota(I32, (CHUNK, CHUNK), 1)
    masks = (row >= col, row <= col)
    tris = tuple(jnp.where(m, 1.0, 0.0).astype(BF16) for m in masks)
    edges = (CHUNK - 1, 0)
    f_refs = (ff_ref, fb_ref)
    inv_sqrt_d = HEAD_DIM ** -0.5

    st_ref[...] = jnp.zeros_like(st_ref)

    def chunk_rows(i, d):
        c = i if d == 0 else n_chunks - 1 - i
        return pl.ds(pl.multiple_of(c * CHUNK, CHUNK), CHUNK)

    def prepare(i, d, slot):
        rows = chunk_rows(jnp.minimum(i, n_chunks - 1), d)
        q = q_ref[0, rows, :].astype(F32)
        fr = f_refs[d][0, rows, :].astype(F32)
        lb = lb_ref[d:d + 1, :]
        f = lb + (1.0 - lb) * jax.nn.sigmoid(fr)
        k = 1.0 - f
        hi, mid, lo = _split3(jnp.log(f))
        b = _dot(tris[d], hi) + _dot(tris[d], mid) + _dot(tris[d], lo)
        b_edge = b[edges[d]:edges[d] + 1, :]
        qd_ref[d, slot] = (q * jnp.exp(b) * inv_sqrt_d).astype(BF16)
        ki_ref[d, slot] = (k * jnp.exp(-b)).astype(BF16)
        kd_ref[d, slot] = (k * jnp.exp(b_edge - b)).astype(BF16)
        dec_ref[d, slot] = jnp.broadcast_to(jnp.exp(b_edge), dec_ref.shape[2:])

    def advance(i, d, slot):
        rows = chunk_rows(i, d)
        v = v_ref[0, rows, :]
        q_dec, k_inv, k_dec = qd_ref[d, slot], ki_ref[d, slot], kd_ref[d, slot]
        dec = dec_ref[d, slot, 0:1, :]
        outs = []
        for h in range(heads):
            sl = slice(h * HEAD_DIM, (h + 1) * HEAD_DIM)
            st = st_ref[d, h]
            scores = jnp.where(masks[d], _dot_nt(q_dec[:, sl], k_inv[:, sl]), 0.0).astype(BF16)
            outs.append(_dot(scores, v[:, sl]) + _dot_nt(q_dec[:, sl], st.astype(BF16)))
            st_ref[d, h] = st * dec[:, sl] + _dot_tn(v[:, sl], k_dec[:, sl])
        return rows, outs

    def first_touch(rows, outs):
        for h in range(heads):
            acc_ref[rows, h * HEAD_DIM:(h + 1) * HEAD_DIM] = outs[h]

    def finish(rows, outs):
        og = og_ref[0, rows, :].astype(F32)
        for h in range(heads):
            sl = slice(h * HEAD_DIM, (h + 1) * HEAD_DIM)
            o = outs[h] + acc_ref[rows, sl]
            gate = og[:, sl] * jax.nn.sigmoid(og[:, sl])
            o_ref[0, rows, sl] = (_rms(o, go_ref[...]) * gate).astype(o_ref.dtype)

    def two_steps(sink):
        def body(j, carry):
            i = 2 * j
            for slot in (0, 1):
                for d in (0, 1):
                    prepare(i + slot + 1, d, 1 - slot)
                for d in (0, 1):
                    sink(*advance(i + slot, d, slot))
            return carry
        return body

    for d in (0, 1):
        prepare(0, d, 0)
    lax.fori_loop(0, half // 2, two_steps(first_touch), 0)
    lax.fori_loop(half // 2, n_chunks // 2, two_steps(finish), 0)


def _hgrn(z3, lb, g_o, heads):
    B, S, _ = z3.shape
    assert (S // CHUNK) % 2 == 0 and HGRN_HEADS % heads == 0
    W = heads * HEAD_DIM
    nblk = HGRN_WIDTH // W
    base = FOURIER_WIDTH // W

    def zspec(j):
        return pl.BlockSpec((1, S, W), lambda b, h, j=j: (b, 0, base + j * nblk + h))

    return pl.pallas_call(
        functools.partial(_hgrn_kernel, seq=S, heads=heads),
        out_shape=jax.ShapeDtypeStruct((B, S, HGRN_WIDTH), BF16),
        grid=(B, nblk),
        in_specs=[zspec(0), zspec(1), zspec(2), zspec(3), zspec(4),
                  pl.BlockSpec((2, W), lambda b, h: (0, h)),
                  pl.BlockSpec((1, HEAD_DIM), lambda b, h: (0, 0))],
        out_specs=pl.BlockSpec((1, S, W), lambda b, h: (b, 0, h)),
        scratch_shapes=[pltpu.VMEM((S, W), F32),
                        pltpu.VMEM((2, heads, HEAD_DIM, HEAD_DIM), F32),
                        pltpu.VMEM((2, 2, CHUNK, W), BF16),
                        pltpu.VMEM((2, 2, CHUNK, W), BF16),
                        pltpu.VMEM((2, 2, CHUNK, W), BF16),
                        pltpu.VMEM((2, 2, SUBLANES, W), F32)],
        compiler_params=_params("parallel", "arbitrary"),
        name="hgrn",
    )(z3, z3, z3, z3, z3, lb, g_o)


def _outproj_kernel(x_ref, yf_ref, o_ref, wa_ref, wb_ref, g_ref, wr_ref, tri_ref, cnt_in_ref,
                    x1_ref, slab_ref, meta_ref, counts_ref, cnt_ref):
    @pl.when(pl.program_id(0) == 0)
    def _():
        cnt_ref[...] = cnt_in_ref[...]

    tile = x_ref.shape[0]
    x1 = x_ref[...] + _dot(yf_ref[...], wa_ref[...]) + _dot(o_ref[...], wb_ref[...])
    x1_ref[...] = x1
    hb = _rms(x1, g_ref[...]).astype(BF16)
    logits = _dot(hb, wr_ref[...])
    lane = lax.broadcasted_iota(I32, logits.shape, 1)
    neg = -jnp.inf

    def argmax_lowest(x, m):
        return jnp.min(jnp.where(x == m, lane, LANES), axis=1, keepdims=True)

    lg = jnp.where((lane >= N_EXPERTS) & (lane < N_EXPERTS + N_GROUPS), logits, neg)
    mg = jnp.max(lg, axis=1, keepdims=True)
    p_top = 1.0 / jnp.sum(jnp.exp(lg - mg), axis=1, keepdims=True)
    g_idx = argmax_lowest(lg, mg) - N_EXPERTS
    base = g_idx * EXPERTS_PER_GROUP
    le = jnp.where((lane >= base) & (lane < base + EXPERTS_PER_GROUP), logits, neg)
    v1 = jnp.max(le, axis=1, keepdims=True)
    i1 = argmax_lowest(le, v1)
    le2 = jnp.where(lane == i1, neg, le)
    v2 = jnp.max(le2, axis=1, keepdims=True)
    i2 = argmax_lowest(le2, v2)
    e2 = jnp.exp(v2 - v1)
    w1 = p_top / (1.0 + e2)
    w2 = w1 * e2

    a = jnp.minimum(i1, i2) - base
    b = jnp.maximum(i1, i2) - base
    pair = lax.shift_right_logical(a * (2 * EXPERTS_PER_GROUP - 1 - a), 1) + (b - a - 1)
    cls = g_idx * PAIRS_PER_GROUP + pair
    first_is_lo = i1 < i2
    w_lo = jnp.where(first_is_lo, w1, w2)
    w_hi = jnp.where(first_is_lo, w2, w1)

    onehot = jnp.where(lane == cls, 1.0, 0.0)
    before = _dot(tri_ref[...], onehot.astype(BF16))
    carry = cnt_ref[0:1, :]
    rank = jnp.sum(onehot * (carry + before), axis=1, keepdims=True)
    new_counts = jnp.broadcast_to(carry + jnp.sum(onehot, axis=0, keepdims=True), cnt_ref.shape)
    cnt_ref[...] = new_counts
    counts_ref[...] = new_counts

    def bf16_pair(w):
        w = jnp.broadcast_to(w, (tile, LANES))
        head = w.astype(BF16)
        return [head, (w - head.astype(F32)).astype(BF16)]

    slab = jnp.concatenate([hb] + bf16_pair(w_lo) + bf16_pair(w_hi)
                           + [jnp.zeros((tile, SLAB_WIDTH - D_MODEL - 4 * LANES), BF16)], axis=1)
    slab_ref[...] = slab.reshape((tile,) + SLAB)

    meta = jnp.where(lane == META_CLASS, cls, jnp.where(lane == META_RANK, rank.astype(I32), 0))
    meta_ref[...] = meta.T[0:SUBLANES, :]


def _outproj(x2d, yf2d, o2d, wa, wb, g, wr, counts_in, tile):
    T = x2d.shape[0]
    tri = jnp.tril(jnp.ones((tile, tile), F32), -1).astype(BF16)
    tok = lambda w: pl.BlockSpec((tile, w), lambda i: (i, 0))
    full = lambda a: pl.BlockSpec(a.shape, lambda i: (0,) * a.ndim)
    return pl.pallas_call(
        _outproj_kernel,
        out_shape=(jax.ShapeDtypeStruct((T, D_MODEL), F32),
                   jax.ShapeDtypeStruct((T,) + SLAB, SLAB_DTYPE),
                   jax.ShapeDtypeStruct((SUBLANES, T), I32),
                   jax.ShapeDtypeStruct((SUBLANES, LANES), F32)),
        grid=(T // tile,),
        in_specs=[tok(D_MODEL), tok(FOURIER_WIDTH), tok(HGRN_WIDTH), full(wa), full(wb), full(g), full(wr),
                  full(tri), full(counts_in)],
        out_specs=(tok(D_MODEL), pl.BlockSpec((tile,) + SLAB, lambda i: (i,) + SLAB_ZEROS),
                   pl.BlockSpec((SUBLANES, tile), lambda i: (0, i)),
                   pl.BlockSpec((SUBLANES, LANES), lambda i: (0, 0))),
        scratch_shapes=[pltpu.VMEM((SUBLANES, LANES), F32)],
        compiler_params=_params("arbitrary"),
        name="outproj_router",
    )(x2d, yf2d, o2d, wa, wb, g, wr, tri, counts_in)


def _positions_kernel(starts_ref, meta_ref, pos_ref):
    cls = meta_ref[META_CLASS:META_CLASS + 1, :]
    pos = meta_ref[META_RANK:META_RANK + 1, :]
    for c in range(N_CLASSES):
        pos = pos + jnp.where(cls == c, starts_ref[c], 0)
    pos_ref[...] = jnp.broadcast_to(pos, pos_ref.shape)


def _positions(starts, meta, tile=4096):
    T = meta.shape[1]
    tile = min(tile, T)
    block = pl.BlockSpec((SUBLANES, tile), lambda i, s: (0, i))
    return pl.pallas_call(
        _positions_kernel,
        out_shape=jax.ShapeDtypeStruct(meta.shape, I32),
        grid_spec=pltpu.PrefetchScalarGridSpec(num_scalar_prefetch=1, grid=(T // tile,),
                                               in_specs=[block], out_specs=block),
        compiler_params=_params("parallel"),
        name="moe_positions",
    )(starts, meta)


def _sorted_row(pos_ref, t):
    return pos_ref[0, t]


ROW_COPY_UNROLL = 8


def _start_row_copies(n, make):
    assert n % ROW_COPY_UNROLL == 0

    def body(j, carry):
        for u in range(ROW_COPY_UNROLL):
            make(j * ROW_COPY_UNROLL + u).start()
        return carry
    lax.fori_loop(0, n // ROW_COPY_UNROLL, body, 0)


def _dispatch_kernel(fill_ref, *refs, bounds):
    n_trunks = len(bounds) - 1
    meta_refs, slab_refs = refs[0:2 * n_trunks:2], refs[1:2 * n_trunks:2]
    xs_hbm, zero_ref, sem = refs[2 * n_trunks:]
    i = pl.program_id(0)
    ftile = zero_ref.shape[0]

    @pl.when(i == 0)
    def _():
        zero_ref[...] = jnp.zeros_like(zero_ref)
        fill = lambda j: pltpu.make_async_copy(
            zero_ref, xs_hbm.at[pl.ds(pl.multiple_of(fill_ref[j] * ftile, ftile), ftile)], sem.at[1])

        def start(j, carry):
            @pl.when(fill_ref[j] >= 0)
            def _():
                fill(j).start()
            return carry

        def wait(j, carry):
            @pl.when(fill_ref[j] >= 0)
            def _():
                fill(j).wait()
            return carry

        lax.fori_loop(0, fill_ref.shape[0], start, 0)
        lax.fori_loop(0, fill_ref.shape[0], wait, 0)

    for k in range(n_trunks):
        @pl.when((i >= bounds[k]) & (i < bounds[k + 1]))
        def _(meta_ref=meta_refs[k], slab_ref=slab_refs[k]):
            tile = slab_ref.shape[0]
            _start_row_copies(tile, lambda t: pltpu.make_async_copy(
                slab_ref.at[t], xs_hbm.at[_sorted_row(meta_ref, t)], sem.at[0]))
            pltpu.make_async_copy(slab_ref, xs_hbm.at[pl.ds(0, tile)], sem.at[0]).wait()


def _dispatch(fill_tiles, metas, slabs, n_rows, tile):
    steps = [slab.shape[0] // tile for slab in slabs]
    bounds = tuple(int(b) for b in np.cumsum([0] + steps))
    in_specs, args = [], []
    for k, (meta, slab) in enumerate(zip(metas, slabs)):
        local = lambda i, k=k: jnp.clip(i - bounds[k], 0, steps[k] - 1)
        in_specs.append(pl.BlockSpec((SUBLANES, tile), lambda i, f, local=local: (0, local(i)),
                                     memory_space=pltpu.SMEM))
        in_specs.append(pl.BlockSpec((tile,) + SLAB, lambda i, f, local=local: (local(i),) + SLAB_ZEROS))
        args += [meta, slab]
    return pl.pallas_call(
        functools.partial(_dispatch_kernel, bounds=bounds),
        out_shape=jax.ShapeDtypeStruct((n_rows,) + SLAB, SLAB_DTYPE),
        grid_spec=pltpu.PrefetchScalarGridSpec(
            num_scalar_prefetch=1, grid=(bounds[-1],),
            in_specs=in_specs,
            out_specs=pl.BlockSpec(memory_space=pl.ANY),
            scratch_shapes=[pltpu.VMEM((EXPERT_TILE,) + SLAB, SLAB_DTYPE), pltpu.SemaphoreType.DMA((2,))]),
        compiler_params=_params("arbitrary"),
        name="moe_dispatch",
    )(fill_tiles, *args)


def _expert_kernel(elo_ref, ehi_ref, rows_ref, xs_ref, wg0_ref, wu0_ref, wd0_ref, wg1_ref, wu1_ref, wd1_ref,
                   ys_ref):
    i = pl.program_id(0)
    tile = xs_ref.shape[0]
    valid = rows_ref[i]

    @pl.when(valid > 0)
    def _():
        slab = xs_ref[...].reshape(tile, SLAB_WIDTH)
        x = slab[:, :D_MODEL]
        chunk = lambda j: slab[:, D_MODEL + j * LANES:D_MODEL + (j + 1) * LANES].astype(F32)
        y = None
        for k, (wg_ref, wu_ref, wd_ref) in enumerate(((wg0_ref, wu0_ref, wd0_ref), (wg1_ref, wu1_ref, wd1_ref))):
            a = _dot(x, wg_ref[0])
            hid = (a * jax.nn.sigmoid(a) * _dot(x, wu_ref[0])).astype(BF16)
            weight = jnp.tile(chunk(2 * k) + chunk(2 * k + 1), (1, D_MODEL // LANES))
            part = weight * _dot(hid, wd_ref[0])
            y = part if y is None else y + part
        ys_ref[...] = y.reshape(ys_ref.shape)

    @pl.when(valid == 0)
    def _():
        ys_ref[...] = jnp.zeros_like(ys_ref)


def _experts(tile_elo, tile_ehi, tile_rows, xs, wg, wu, wd, tile):
    n_tiles = xs.shape[0] // tile
    lo = lambda a: pl.BlockSpec((1,) + a.shape[1:], lambda i, elo, ehi, rows: (elo[i], 0, 0))
    hi = lambda a: pl.BlockSpec((1,) + a.shape[1:], lambda i, elo, ehi, rows: (ehi[i], 0, 0))
    return pl.pallas_call(
        _expert_kernel,
        out_shape=jax.ShapeDtypeStruct((xs.shape[0],) + ROW, F32),
        grid_spec=pltpu.PrefetchScalarGridSpec(
            num_scalar_prefetch=3, grid=(n_tiles,),
            in_specs=[pl.BlockSpec((tile,) + SLAB, lambda i, elo, ehi, rows: (i,) + SLAB_ZEROS),
                      lo(wg), lo(wu), lo(wd), hi(wg), hi(wu), hi(wd)],
            out_specs=pl.BlockSpec((tile,) + ROW, lambda i, elo, ehi, rows: (i, 0, 0))),
        compiler_params=_params("arbitrary"),
        name="moe_experts",
    )(tile_elo, tile_ehi, tile_rows, xs, wg, wu, wd, wg, wu, wd)


def _tail_kernel(meta_ref, meta_next_ref, ys_hbm, x1_ref, p_ref, gp_ref, wpg_ref, wpp_ref, gf_ref,
                 out_ref, buf_ref, buf2_ref, sem, *, n):
    i = pl.program_id(0)
    tile = x1_ref.shape[0]
    bufs = (buf_ref, buf2_ref)

    def row_copy(m_ref, slot, t):
        return pltpu.make_async_copy(ys_hbm.at[_sorted_row(m_ref, t)], bufs[slot].at[t], sem.at[slot])

    def wait_tile(slot):
        pltpu.make_async_copy(ys_hbm.at[pl.ds(0, tile)], bufs[slot], sem.at[slot]).wait()

    @pl.when(i == 0)
    def _():
        _start_row_copies(tile, lambda t: row_copy(meta_ref, 0, t))

    def step(slot):
        wait_tile(slot)
        for t in range(tile):
            row_copy(meta_next_ref, 1 - slot, t).start()
        ple = _dot(p_ref[...].astype(BF16), wpp_ref[...])
        x2 = x1_ref[...] + bufs[slot][...].reshape(tile, D_MODEL)
        hp = _rms(x2, gp_ref[...]).astype(BF16)
        gate = jax.nn.sigmoid(_dot(hp, wpg_ref[...]))
        out_ref[...] = _rms(x2 + ple * gate, gf_ref[...])

        @pl.when(i == n - 1)
        def _():
            wait_tile(1 - slot)

    for slot in (0, 1):
        pl.when(i % 2 == slot)(functools.partial(step, slot))


def _tail(meta, ys, x1, p2d, gp, wpg, wpp, gf, tile):
    T = x1.shape[0]
    n = T // tile
    tok = lambda w: pl.BlockSpec((tile, w), lambda i: (i, 0))
    full = lambda a: pl.BlockSpec(a.shape, lambda i: (0,) * a.ndim)
    return pl.pallas_call(
        functools.partial(_tail_kernel, n=n),
        out_shape=jax.ShapeDtypeStruct((T, D_MODEL), F32),
        grid_spec=pltpu.PrefetchScalarGridSpec(
            num_scalar_prefetch=0, grid=(n,),
            in_specs=[pl.BlockSpec((SUBLANES, tile), lambda i: (0, i), memory_space=pltpu.SMEM),
                      pl.BlockSpec((SUBLANES, tile), lambda i: (0, jnp.minimum(i + 1, n - 1)),
                                   memory_space=pltpu.SMEM),
                      pl.BlockSpec(memory_space=pl.ANY), tok(D_MODEL), tok(PLE_DIM),
                      full(gp), full(wpg), full(wpp), full(gf)],
            out_specs=tok(D_MODEL),
            scratch_shapes=[pltpu.VMEM((tile,) + ROW, F32), pltpu.VMEM((tile,) + ROW, F32),
                            pltpu.SemaphoreType.DMA((2,))]),
        compiler_params=_params("arbitrary"),
        name="combine_tail",
    )(meta, meta, ys, x1, p2d, gp, wpg, wpp, gf)


def _class_experts():
    lo, hi = [], []
    for g in range(N_GROUPS):
        for a in range(EXPERTS_PER_GROUP):
            for b in range(a + 1, EXPERTS_PER_GROUP):
                lo.append(g * EXPERTS_PER_GROUP + a)
                hi.append(g * EXPERTS_PER_GROUP + b)
    return np.asarray(lo, np.int32), np.asarray(hi, np.int32)


def _sorted_layout(counts, n_tiles, tile):
    cnt = counts[0, :N_CLASSES].astype(I32)
    padded = (cnt + tile - 1) // tile * tile
    ends = jnp.cumsum(padded)
    starts = ends - padded
    tile_start = jnp.arange(n_tiles, dtype=I32) * tile
    tile_class = jnp.minimum(jnp.sum((ends[None, :] <= tile_start[:, None]).astype(I32), axis=1), N_CLASSES - 1)
    onehot = (tile_class[:, None] == jnp.arange(N_CLASSES, dtype=I32)[None, :]).astype(I32)
    pick = lambda table: jnp.sum(onehot * table[None, :], axis=1).astype(I32)
    tile_rows = jnp.clip(pick(cnt) - (tile_start - pick(starts)), 0, tile)
    tile_rows = jnp.where(tile_start < ends[-1], tile_rows, 0).astype(I32)
    class_lo, class_hi = _class_experts()
    partial = jnp.where(padded > cnt, ends // tile - 1, -1)
    tail = ends[-1] // tile + jnp.arange(N_CLASSES, dtype=I32)
    fill_tiles = jnp.concatenate([partial, jnp.where(tail < n_tiles, tail, -1)]).astype(I32)
    starts_padded = jnp.zeros((LANES,), I32).at[:N_CLASSES].set(starts)
    return starts_padded, pick(jnp.asarray(class_lo)), pick(jnp.asarray(class_hi)), tile_rows, fill_tiles


def _mix_and_route(x, w, counts_in):
    B, S, _ = x.shape
    T = B * S
    x2d = x.reshape(T, D_MODEL)
    z = _inproj(x2d, w["g_mix"], w["w_in"], tile=512)
    z3 = z.reshape(B, S, IN_WIDTH)
    yf = _fourier(z3, w["w_fourier"])
    o = _hgrn(z3, w["lb"], w["g_o"], heads=4 if S <= 2048 else 2)
    return _outproj(x2d, yf.reshape(T, FOURIER_WIDTH), o.reshape(T, HGRN_WIDTH),
                    w["w_out_a"], w["w_out_b"], w["g_ffn"], w["w_router"], counts_in, tile=ROUTE_TILE)


def kernel(x_prompt, x_sample, p_prompt, p_sample, norm_mix, w_in, w_fourier, lb_logits, norm_o, w_out,
           norm_ffn, w_route_group, w_route_expert, w_exp_gate, w_exp_up, w_exp_down, norm_ple,
           w_ple_gate, w_ple_proj, norm_final):
    assert w_in.shape[0] == 1, "single-layer trunk"
    lb_all = jnp.cumsum(jax.nn.softmax(lb_logits.astype(F32), axis=0), axis=0)
    router = jnp.concatenate(
        [w_route_expert[0], w_route_group[0],
         jnp.zeros((D_MODEL, LANES - N_EXPERTS - N_GROUPS), F32)], axis=1)
    w = {
        "g_mix": norm_mix[0][None, :],
        "w_in": w_in[0].astype(BF16),
        "w_fourier": w_fourier[0],
        "lb": lb_all[0],
        "g_o": norm_o[0][None, :],
        "w_out_a": w_out[0, :FOURIER_WIDTH].astype(BF16),
        "w_out_b": w_out[0, FOURIER_WIDTH:].astype(BF16),
        "g_ffn": norm_ffn[0][None, :],
        "w_router": router.astype(BF16),
    }
    xs_in = (x_prompt, x_sample)
    ps_in = (p_prompt[0], p_sample[0])

    counts = jnp.zeros((SUBLANES, LANES), F32)
    routed = []
    for x in xs_in:
        x1, slab, meta, counts = _mix_and_route(x, w, counts)
        routed.append((x1, slab, meta))

    total = sum(x.shape[0] * x.shape[1] for x in xs_in)
    n_tiles = total // EXPERT_TILE + N_CLASSES
    starts, tile_elo, tile_ehi, tile_rows, fill_tiles = _sorted_layout(counts, n_tiles, EXPERT_TILE)

    positions = [_positions(starts, meta) for _, _, meta in routed]
    xs = _dispatch(fill_tiles, positions, [r[1] for r in routed], n_tiles * EXPERT_TILE, tile=ROUTE_TILE)
    ys = _experts(tile_elo, tile_ehi, tile_rows, xs, w_exp_gate[0].astype(BF16), w_exp_up[0].astype(BF16),
                  w_exp_down[0].astype(BF16), tile=EXPERT_TILE)

    outs = []
    for (x1, _, _), pos, x, p in zip(routed, positions, xs_in, ps_in):
        out = _tail(pos, ys, x1, p.reshape(-1, PLE_DIM), norm_ple[0][None, :],
                    w_ple_gate[0].astype(BF16), w_ple_proj[0].astype(BF16), norm_final[None, :],
                    tile=ROUTE_TILE)
        outs.append(out.reshape(x.shape))
    return tuple(outs)
```

```python
import functools
import math

import numpy as np
import jax
import jax.numpy as jnp
from jax import lax
from jax.experimental import pallas as pl
from jax.experimental.pallas import tpu as pltpu

F32 = jnp.float32
BF16 = jnp.bfloat16
I32 = jnp.int32

D_MODEL = 1024
FOURIER_WIDTH = 512
FOURIER_GROUPS = 4
GROUP_DIM = 128
HGRN_WIDTH = 512
HEAD_DIM = 128
HGRN_HEADS = 4
CHUNK = 64
IN_WIDTH = FOURIER_WIDTH + 5 * HGRN_WIDTH
N_GROUPS = 4
EXPERTS_PER_GROUP = 8
N_EXPERTS = 32
PAIRS_PER_GROUP = EXPERTS_PER_GROUP * (EXPERTS_PER_GROUP - 1) // 2
N_CLASSES = N_GROUPS * PAIRS_PER_GROUP
D_EXPERT = 512
PLE_DIM = 256
EPS = 1e-6
LANES = 128
SUBLANES = 8
VMEM_LIMIT = 56 * 1024 * 1024
ROUTE_TILE = 512
EXPERT_TILE = 256
HGRN_STEPS_PER_TRIP = 4
ROW = (SUBLANES, LANES)
SLAB = (2 * SUBLANES, LANES)
SLAB_DTYPE = BF16
SLAB_WIDTH = 2 * D_MODEL
SLAB_ZEROS = (0,) * len(SLAB)
META_CLASS, META_RANK = 0, 1

assert N_CLASSES <= LANES


def _params(*sem):
    return pltpu.CompilerParams(dimension_semantics=sem, vmem_limit_bytes=VMEM_LIMIT)


def _dot(a, b):
    return jnp.dot(a, b, preferred_element_type=F32)


def _dot_nt(a, b):
    return lax.dot_general(a, b, (((1,), (1,)), ((), ())), preferred_element_type=F32)


def _dot_tn(a, b):
    return lax.dot_general(a, b, (((0,), (0,)), ((), ())), preferred_element_type=F32)


def _rms(x, g):
    return x * lax.rsqrt(jnp.mean(x * x, axis=-1, keepdims=True) + EPS) * g


def _inproj_kernel(x_ref, g_ref, w_ref, z_ref):
    h = _rms(x_ref[...], g_ref[...]).astype(BF16)
    for j in range(0, IN_WIDTH, D_MODEL):
        z_ref[:, j:j + D_MODEL] = _dot(h, w_ref[:, j:j + D_MODEL]).astype(z_ref.dtype)


def _inproj(x2d, g, w_bf, tile):
    T = x2d.shape[0]
    return pl.pallas_call(
        _inproj_kernel,
        out_shape=jax.ShapeDtypeStruct((T, IN_WIDTH), BF16),
        grid=(T // tile,),
        in_specs=[
            pl.BlockSpec((tile, D_MODEL), lambda i: (i, 0)),
            pl.BlockSpec((1, D_MODEL), lambda i: (0, 0)),
            pl.BlockSpec((D_MODEL, IN_WIDTH), lambda i: (0, 0)),
        ],
        out_specs=pl.BlockSpec((tile, IN_WIDTH), lambda i: (i, 0)),
        compiler_params=_params("parallel"),
        name="inproj",
    )(x2d, g, w_bf)


def _fourier_fold_kernel(cc_ref, sc_ref, w_ref, m_ref, *, scale):
    for g in range(FOURIER_GROUPS):
        w = w_ref[g]
        m_ref[g, :GROUP_DIM, :] = (jnp.dot(cc_ref[...], w, preferred_element_type=F32,
                                           precision=lax.Precision.HIGHEST) * scale).astype(m_ref.dtype)
        m_ref[g, GROUP_DIM:, :] = (jnp.dot(sc_ref[...], w, preferred_element_type=F32,
                                           precision=lax.Precision.HIGHEST) * (-scale)).astype(m_ref.dtype)


def _fourier_fold(w_fourier, seq):
    cc, sc = _dft_tables(GROUP_DIM, GROUP_DIM, 1, 0, GROUP_DIM, F32)
    scale = 1.0 / math.sqrt(seq * GROUP_DIM)
    return pl.pallas_call(
        functools.partial(_fourier_fold_kernel, scale=scale),
        out_shape=jax.ShapeDtypeStruct((FOURIER_GROUPS, 2 * GROUP_DIM, GROUP_DIM), BF16),
        name="fourier_fold",
    )(cc, sc, w_fourier)


def _dft_tables(n_rows, n_cols, row_mult, row_off, period, dtype):
    r = int(round(math.sqrt(n_rows)))
    while n_rows % r:
        r -= 1
    m = jnp.arange(n_cols, dtype=I32)[None, :]
    k_lo = jnp.arange(r, dtype=I32)[:, None] * row_mult + row_off
    k_hi = jnp.arange(n_rows // r, dtype=I32)[:, None] * (r * row_mult)
    step = 2.0 * math.pi / period
    a_lo = (((k_lo % period) * m) % period).astype(F32) * step
    a_hi = (((k_hi % period) * m) % period).astype(F32) * step
    c_lo, s_lo = jnp.cos(a_lo)[None], jnp.sin(a_lo)[None]
    c_hi, s_hi = jnp.cos(a_hi)[:, None], jnp.sin(a_hi)[:, None]
    cos = (c_hi * c_lo - s_hi * s_lo).reshape(n_rows, n_cols)
    sin = (s_hi * c_lo + c_hi * s_lo).reshape(n_rows, n_cols)
    return cos.astype(dtype), sin.astype(dtype)


def _fourier_kernel(ulo_ref, uhi_ref, ce_ref, se_ref, co_ref, so_ref, m_ref, y_ref, sum_ref, dif_ref, stage_ref):
    tk = ce_ref.shape[0]

    @pl.when(pl.program_id(1) == 0)
    def _():
        lo = ulo_ref[0].astype(F32)
        hi = uhi_ref[0].astype(F32)
        sum_ref[...] = (lo + hi).astype(BF16)
        dif_ref[...] = (lo - hi).astype(BF16)

    def mix(cos_ref, sin_ref, x, parity):
        a = _dot(cos_ref[...], x).astype(BF16)
        b = _dot(sin_ref[...], x).astype(BF16)
        for g in range(FOURIER_GROUPS):
            sl = slice(g * GROUP_DIM, (g + 1) * GROUP_DIM)
            stage_ref[g, pl.ds(parity, tk, stride=2), :] = _dot(
                jnp.concatenate([a[:, sl], b[:, sl]], axis=1), m_ref[g])

    mix(ce_ref, se_ref, sum_ref[...], 0)
    mix(co_ref, so_ref, dif_ref[...], 1)
    for g in range(FOURIER_GROUPS):
        y_ref[0, :, g * GROUP_DIM:(g + 1) * GROUP_DIM] = stage_ref[g].astype(y_ref.dtype)


def _fourier(z3, w_fourier, tk=512):
    B, S, _ = z3.shape
    M = S // 2
    tk = min(tk, M)
    ce, se = _dft_tables(M, M, 1, 0, M, BF16)
    co, so = _dft_tables(M, M, 2, 1, S, BF16)
    m = _fourier_fold(w_fourier, S)
    table = pl.BlockSpec((tk, M), lambda b, k: (k, 0))
    return pl.pallas_call(
        _fourier_kernel,
        out_shape=jax.ShapeDtypeStruct((B, S, FOURIER_WIDTH), BF16),
        grid=(B, M // tk),
        in_specs=[
            pl.BlockSpec((1, M, FOURIER_WIDTH), lambda b, k: (b, 0, 0)),
            pl.BlockSpec((1, M, FOURIER_WIDTH), lambda b, k: (b, 1, 0)),
            table, table, table, table,
            pl.BlockSpec((FOURIER_GROUPS, 2 * GROUP_DIM, GROUP_DIM), lambda b, k: (0, 0, 0)),
        ],
        out_specs=pl.BlockSpec((1, 2 * tk, FOURIER_WIDTH), lambda b, k: (b, k, 0)),
        scratch_shapes=[pltpu.VMEM((M, FOURIER_WIDTH), BF16), pltpu.VMEM((M, FOURIER_WIDTH), BF16),
                        pltpu.VMEM((FOURIER_GROUPS, 2 * tk, GROUP_DIM), F32)],
        compiler_params=_params("parallel", "arbitrary"),
        name="fourier",
    )(z3, z3, ce, se, co, so, m)


def _scale_query_columns(w_in):
    col = jnp.arange(IN_WIDTH)
    is_q = (col >= FOURIER_WIDTH) & (col < FOURIER_WIDTH + HGRN_WIDTH)
    return w_in * jnp.where(is_q, HEAD_DIM ** -0.5, 1.0).astype(w_in.dtype)[None, :]


def _split3(x):
    hi = x.astype(BF16)
    r1 = x - hi.astype(F32)
    mid = r1.astype(BF16)
    lo = (r1 - mid.astype(F32)).astype(BF16)
    return hi, mid, lo


def _hgrn_kernel(q_ref, v_ref, ff_ref, fb_ref, og_ref, lb_ref, go_ref, o_ref,
                 acc_ref, st_ref, qd_ref, ki_ref, kd_ref, dec_ref, *, seq, heads):
    n_chunks = seq // CHUNK
    half = n_chunks // 2
    assert HGRN_STEPS_PER_TRIP % 2 == 0 and half % HGRN_STEPS_PER_TRIP == 0
    row = lax.broadcasted_iota(I32, (CHUNK, CHUNK), 0)
    col = lax.broadcasted_iota(I32, (CHUNK, CHUNK), 1)
    masks = (row >= col, row <= col)
    tris = tuple(jnp.where(m, 1.0, 0.0).astype(BF16) for m in masks)
    edges = (CHUNK - 1, 0)
    f_refs = (ff_ref, fb_ref)

    st_ref[...] = jnp.zeros_like(st_ref)

    def chunk_rows(i, d):
        c = i if d == 0 else n_chunks - 1 - i
        return pl.ds(pl.multiple_of(c * CHUNK, CHUNK), CHUNK)

    def prepare(i, d, slot):
        rows = chunk_rows(jnp.minimum(i, n_chunks - 1), d)
        q = q_ref[0, rows, :].astype(F32)
        fr = f_refs[d][0, rows, :].astype(F32)
        lb = lb_ref[d:d + 1, :]
        f = lb + (1.0 - lb) * jax.nn.sigmoid(fr)
        k = 1.0 - f
        hi, mid, lo = _split3(jnp.log(f))
        b = _dot(tris[d], hi) + _dot(tris[d], mid) + _dot(tris[d], lo)
        b_edge = b[edges[d]:edges[d] + 1, :]
        qd_ref[d, slot] = (q * jnp.exp(b)).astype(BF16)
        ki_ref[d, slot] = (k * jnp.exp(-b)).astype(BF16)
        kd_ref[d, slot] = (k * jnp.exp(b_edge - b)).astype(BF16)
        dec_ref[d, slot] = jnp.broadcast_to(jnp.exp(b_edge), dec_ref.shape[2:])

    def advance(i, d, slot):
        rows = chunk_rows(i, d)
        v = v_ref[0, rows, :]
        q_dec, k_inv, k_dec = qd_ref[d, slot], ki_ref[d, slot], kd_ref[d, slot]
        dec = dec_ref[d, slot, 0:1, :]
        outs = []
        for h in range(heads):
            sl = slice(h * HEAD_DIM, (h + 1) * HEAD_DIM)
            st = st_ref[d, h]
            scores = jnp.where(masks[d], _dot_nt(q_dec[:, sl], k_inv[:, sl]), 0.0).astype(BF16)
            outs.append(_dot(scores, v[:, sl]) + _dot_nt(q_dec[:, sl], st.astype(BF16)))
            st_ref[d, h] = st * dec[:, sl] + _dot_tn(v[:, sl], k_dec[:, sl])
        return rows, outs

    def first_touch(rows, outs):
        for h in range(heads):
            acc_ref[rows, h * HEAD_DIM:(h + 1) * HEAD_DIM] = outs[h]

    def finish(rows, outs):
        og = og_ref[0, rows, :].astype(F32)
        for h in range(heads):
            sl = slice(h * HEAD_DIM, (h + 1) * HEAD_DIM)
            o = outs[h] + acc_ref[rows, sl]
            gate = og[:, sl] * jax.nn.sigmoid(og[:, sl])
            o_ref[0, rows, sl] = (_rms(o, go_ref[...]) * gate).astype(o_ref.dtype)

    def steps(sink):
        def body(j, carry):
            for u in range(HGRN_STEPS_PER_TRIP):
                i, slot = HGRN_STEPS_PER_TRIP * j + u, u % 2
                for d in (0, 1):
                    prepare(i + 1, d, 1 - slot)
                for d in (0, 1):
                    sink(*advance(i, d, slot))
            return carry
        return body

    for d in (0, 1):
        prepare(0, d, 0)
    lax.fori_loop(0, half // HGRN_STEPS_PER_TRIP, steps(first_touch), 0)
    lax.fori_loop(half // HGRN_STEPS_PER_TRIP, n_chunks // HGRN_STEPS_PER_TRIP, steps(finish), 0)


def _hgrn(z3, lb, g_o, heads):
    B, S, _ = z3.shape
    assert (S // CHUNK) % 2 == 0 and HGRN_HEADS % heads == 0
    W = heads * HEAD_DIM
    nblk = HGRN_WIDTH // W
    base = FOURIER_WIDTH // W

    def zspec(j):
        return pl.BlockSpec((1, S, W), lambda b, h, j=j: (b, 0, base + j * nblk + h))

    return pl.pallas_call(
        functools.partial(_hgrn_kernel, seq=S, heads=heads),
        out_shape=jax.ShapeDtypeStruct((B, S, HGRN_WIDTH), BF16),
        grid=(B, nblk),
        in_specs=[zspec(0), zspec(1), zspec(2), zspec(3), zspec(4),
                  pl.BlockSpec((2, W), lambda b, h: (0, h)),
                  pl.BlockSpec((1, HEAD_DIM), lambda b, h: (0, 0))],
        out_specs=pl.BlockSpec((1, S, W), lambda b, h: (b, 0, h)),
        scratch_shapes=[pltpu.VMEM((S, W), F32),
                        pltpu.VMEM((2, heads, HEAD_DIM, HEAD_DIM), F32),
                        pltpu.VMEM((2, 2, CHUNK, W), BF16),
                        pltpu.VMEM((2, 2, CHUNK, W), BF16),
                        pltpu.VMEM((2, 2, CHUNK, W), BF16),
                        pltpu.VMEM((2, 2, SUBLANES, W), F32)],
        compiler_params=_params("parallel", "arbitrary"),
        name="hgrn",
    )(z3, z3, z3, z3, z3, lb, g_o)


def _outproj_kernel(x_ref, yf_ref, o_ref, wa_ref, wb_ref, g_ref, wr_ref, tri_ref, cnt_in_ref,
                    x1_ref, slab_ref, meta_ref, counts_ref, cnt_ref):
    @pl.when(pl.program_id(0) == 0)
    def _():
        cnt_ref[...] = cnt_in_ref[...]

    tile = x_ref.shape[0]
    x1 = x_ref[...] + _dot(yf_ref[...], wa_ref[...]) + _dot(o_ref[...], wb_ref[...])
    x1_ref[...] = x1
    hb = _rms(x1, g_ref[...]).astype(BF16)
    logits = _dot(hb, wr_ref[...])
    lane = lax.broadcasted_iota(I32, logits.shape, 1)
    neg = -jnp.inf

    lane_f = lane.astype(F32)

    def argmax_lowest(x, m):
        return jnp.min(jnp.where(x == m, lane_f, float(LANES)), axis=1, keepdims=True).astype(I32)

    lg = jnp.where((lane >= N_EXPERTS) & (lane < N_EXPERTS + N_GROUPS), logits, neg)
    mg = jnp.max(lg, axis=1, keepdims=True)
    p_top = 1.0 / jnp.sum(jnp.exp(lg - mg), axis=1, keepdims=True)
    g_idx = argmax_lowest(lg, mg) - N_EXPERTS
    base = g_idx * EXPERTS_PER_GROUP
    le = jnp.where((lane >= base) & (lane < base + EXPERTS_PER_GROUP), logits, neg)
    v1 = jnp.max(le, axis=1, keepdims=True)
    i1 = argmax_lowest(le, v1)
    le2 = jnp.where(lane == i1, neg, le)
    v2 = jnp.max(le2, axis=1, keepdims=True)
    i2 = argmax_lowest(le2, v2)
    e2 = jnp.exp(v2 - v1)
    w1 = p_top / (1.0 + e2)
    w2 = w1 * e2

    a = jnp.minimum(i1, i2) - base
    b = jnp.maximum(i1, i2) - base
    pair = lax.shift_right_logical(a * (2 * EXPERTS_PER_GROUP - 1 - a), 1) + (b - a - 1)
    cls = g_idx * PAIRS_PER_GROUP + pair
    first_is_lo = i1 < i2
    w_lo = jnp.where(first_is_lo, w1, w2)
    w_hi = jnp.where(first_is_lo, w2, w1)

    onehot = jnp.where(lane == cls, 1.0, 0.0)
    before = _dot(tri_ref[...], onehot.astype(BF16))
    carry = cnt_ref[0:1, :]
    rank = jnp.sum(onehot * (carry + before), axis=1, keepdims=True)
    new_counts = jnp.broadcast_to(carry + jnp.sum(onehot, axis=0, keepdims=True), cnt_ref.shape)
    cnt_ref[...] = new_counts
    counts_ref[...] = new_counts

    def bf16_pair(w):
        w = jnp.broadcast_to(w, (tile, LANES))
        head = w.astype(BF16)
        return [head, (w - head.astype(F32)).astype(BF16)]

    slab = jnp.concatenate([hb] + bf16_pair(w_lo) + bf16_pair(w_hi)
                           + [jnp.zeros((tile, SLAB_WIDTH - D_MODEL - 4 * LANES), BF16)], axis=1)
    slab_ref[...] = slab.reshape((tile,) + SLAB)

    meta = jnp.where(lane == META_CLASS, cls, jnp.where(lane == META_RANK, rank.astype(I32), 0))
    meta_ref[...] = meta.T[0:SUBLANES, :]


def _outproj(x2d, yf2d, o2d, wa, wb, g, wr, counts_in, tile):
    T = x2d.shape[0]
    tri = jnp.tril(jnp.ones((tile, tile), F32), -1).astype(BF16)
    tok = lambda w: pl.BlockSpec((tile, w), lambda i: (i, 0))
    full = lambda a: pl.BlockSpec(a.shape, lambda i: (0,) * a.ndim)
    return pl.pallas_call(
        _outproj_kernel,
        out_shape=(jax.ShapeDtypeStruct((T, D_MODEL), F32),
                   jax.ShapeDtypeStruct((T,) + SLAB, SLAB_DTYPE),
                   jax.ShapeDtypeStruct((SUBLANES, T), I32),
                   jax.ShapeDtypeStruct((SUBLANES, LANES), F32)),
        grid=(T // tile,),
        in_specs=[tok(D_MODEL), tok(FOURIER_WIDTH), tok(HGRN_WIDTH), full(wa), full(wb), full(g), full(wr),
                  full(tri), full(counts_in)],
        out_specs=(tok(D_MODEL), pl.BlockSpec((tile,) + SLAB, lambda i: (i,) + SLAB_ZEROS),
                   pl.BlockSpec((SUBLANES, tile), lambda i: (0, i)),
                   pl.BlockSpec((SUBLANES, LANES), lambda i: (0, 0))),
        scratch_shapes=[pltpu.VMEM((SUBLANES, LANES), F32)],
        compiler_params=_params("arbitrary"),
        name="outproj_router",
    )(x2d, yf2d, o2d, wa, wb, g, wr, tri, counts_in)


def _positions_kernel(starts_ref, meta_ref, pos_ref):
    cls = meta_ref[META_CLASS:META_CLASS + 1, :]
    pos = meta_ref[META_RANK:META_RANK + 1, :]
    for c in range(N_CLASSES):
        pos = pos + jnp.where(cls == c, starts_ref[c], 0)
    pos_ref[...] = jnp.broadcast_to(pos, pos_ref.shape)


def _positions(starts, meta, tile=4096):
    T = meta.shape[1]
    tile = min(tile, T)
    block = pl.BlockSpec((SUBLANES, tile), lambda i, s: (0, i))
    return pl.pallas_call(
        _positions_kernel,
        out_shape=jax.ShapeDtypeStruct(meta.shape, I32),
        grid_spec=pltpu.PrefetchScalarGridSpec(num_scalar_prefetch=1, grid=(T // tile,),
                                               in_specs=[block], out_specs=block),
        compiler_params=_params("parallel"),
        name="moe_positions",
    )(starts, meta)


def _sorted_row(pos_ref, t):
    return pos_ref[0, t]


ROW_COPY_UNROLL = 8


def _start_row_copies(n, make):
    assert n % ROW_COPY_UNROLL == 0

    def body(j, carry):
        for u in range(ROW_COPY_UNROLL):
            make(j * ROW_COPY_UNROLL + u).start()
        return carry
    lax.fori_loop(0, n // ROW_COPY_UNROLL, body, 0)


def _dispatch_kernel(fill_ref, *refs, bounds):
    n_trunks = len(bounds) - 1
    meta_refs, slab_refs = refs[0:2 * n_trunks:2], refs[1:2 * n_trunks:2]
    xs_hbm, zero_ref, sem = refs[2 * n_trunks:]
    i = pl.program_id(0)
    ftile = zero_ref.shape[0]

    @pl.when(i == 0)
    def _():
        zero_ref[...] = jnp.zeros_like(zero_ref)
        fill = lambda j: pltpu.make_async_copy(
            zero_ref, xs_hbm.at[pl.ds(pl.multiple_of(fill_ref[j] * ftile, ftile), ftile)], sem.at[1])

        def start(j, carry):
            @pl.when(fill_ref[j] >= 0)
            def _():
                fill(j).start()
            return carry

        def wait(j, carry):
            @pl.when(fill_ref[j] >= 0)
            def _():
                fill(j).wait()
            return carry

        lax.fori_loop(0, fill_ref.shape[0], start, 0)
        lax.fori_loop(0, fill_ref.shape[0], wait, 0)

    for k in range(n_trunks):
        @pl.when((i >= bounds[k]) & (i < bounds[k + 1]))
        def _(meta_ref=meta_refs[k], slab_ref=slab_refs[k]):
            tile = slab_ref.shape[0]
            _start_row_copies(tile, lambda t: pltpu.make_async_copy(
                slab_ref.at[t], xs_hbm.at[_sorted_row(meta_ref, t)], sem.at[0]))
            pltpu.make_async_copy(slab_ref, xs_hbm.at[pl.ds(0, tile)], sem.at[0]).wait()


def _dispatch(fill_tiles, metas, slabs, n_rows, tile):
    steps = [slab.shape[0] // tile for slab in slabs]
    bounds = tuple(int(b) for b in np.cumsum([0] + steps))
    in_specs, args = [], []
    for k, (meta, slab) in enumerate(zip(metas, slabs)):
        local = lambda i, k=k: jnp.clip(i - bounds[k], 0, steps[k] - 1)
        in_specs.append(pl.BlockSpec((SUBLANES, tile), lambda i, f, local=local: (0, local(i)),
                                     memory_space=pltpu.SMEM))
        in_specs.append(pl.BlockSpec((tile,) + SLAB, lambda i, f, local=local: (local(i),) + SLAB_ZEROS))
        args += [meta, slab]
    return pl.pallas_call(
        functools.partial(_dispatch_kernel, bounds=bounds),
        out_shape=jax.ShapeDtypeStruct((n_rows,) + SLAB, SLAB_DTYPE),
        grid_spec=pltpu.PrefetchScalarGridSpec(
            num_scalar_prefetch=1, grid=(bounds[-1],),
            in_specs=in_specs,
            out_specs=pl.BlockSpec(memory_space=pl.ANY),
            scratch_shapes=[pltpu.VMEM((EXPERT_TILE,) + SLAB, SLAB_DTYPE), pltpu.SemaphoreType.DMA((2,))]),
        compiler_params=_params("arbitrary"),
        name="moe_dispatch",
    )(fill_tiles, *args)


def _expert_kernel(elo_ref, ehi_ref, rows_ref, xs_ref, wg0_ref, wu0_ref, wd0_ref, wg1_ref, wu1_ref, wd1_ref,
                   ys_ref):
    i = pl.program_id(0)
    tile = xs_ref.shape[0]
    valid = rows_ref[i]

    @pl.when(valid > 0)
    def _():
        slab = xs_ref[...].reshape(tile, SLAB_WIDTH)
        x = slab[:, :D_MODEL]
        chunk = lambda j: slab[:, D_MODEL + j * LANES:D_MODEL + (j + 1) * LANES].astype(F32)
        y = None
        for k, (wg_ref, wu_ref, wd_ref) in enumerate(((wg0_ref, wu0_ref, wd0_ref), (wg1_ref, wu1_ref, wd1_ref))):
            a = _dot(x, wg_ref[0])
            hid = (a * jax.nn.sigmoid(a) * _dot(x, wu_ref[0])).astype(BF16)
            weight = jnp.tile(chunk(2 * k) + chunk(2 * k + 1), (1, D_MODEL // LANES))
            part = weight * _dot(hid, wd_ref[0])
            y = part if y is None else y + part
        ys_ref[...] = y.reshape(ys_ref.shape)

    @pl.when(valid == 0)
    def _():
        ys_ref[...] = jnp.zeros_like(ys_ref)


def _experts(tile_elo, tile_ehi, tile_rows, xs, wg, wu, wd, tile):
    n_tiles = xs.shape[0] // tile
    lo = lambda a: pl.BlockSpec((1,) + a.shape[1:], lambda i, elo, ehi, rows: (elo[i], 0, 0))
    hi = lambda a: pl.BlockSpec((1,) + a.shape[1:], lambda i, elo, ehi, rows: (ehi[i], 0, 0))
    return pl.pallas_call(
        _expert_kernel,
        out_shape=jax.ShapeDtypeStruct((xs.shape[0],) + ROW, F32),
        grid_spec=pltpu.PrefetchScalarGridSpec(
            num_scalar_prefetch=3, grid=(n_tiles,),
            in_specs=[pl.BlockSpec((tile,) + SLAB, lambda i, elo, ehi, rows: (i,) + SLAB_ZEROS),
                      lo(wg), lo(wu), lo(wd), hi(wg), hi(wu), hi(wd)],
            out_specs=pl.BlockSpec((tile,) + ROW, lambda i, elo, ehi, rows: (i, 0, 0))),
        compiler_params=_params("arbitrary"),
        name="moe_experts",
    )(tile_elo, tile_ehi, tile_rows, xs, wg, wu, wd, wg, wu, wd)


def _tail_kernel(meta_ref, meta_next_ref, ys_hbm, x1_ref, p_ref, gp_ref, wpg_ref, wpp_ref, gf_ref,
                 out_ref, buf_ref, buf2_ref, sem, *, n):
    i = pl.program_id(0)
    tile = x1_ref.shape[0]
    bufs = (buf_ref, buf2_ref)

    def row_copy(m_ref, slot, t):
        return pltpu.make_async_copy(ys_hbm.at[_sorted_row(m_ref, t)], bufs[slot].at[t], sem.at[slot])

    def wait_tile(slot):
        pltpu.make_async_copy(ys_hbm.at[pl.ds(0, tile)], bufs[slot], sem.at[slot]).wait()

    @pl.when(i == 0)
    def _():
        _start_row_copies(tile, lambda t: row_copy(meta_ref, 0, t))

    def step(slot):
        wait_tile(slot)
        for t in range(tile):
            row_copy(meta_next_ref, 1 - slot, t).start()
        ple = _dot(p_ref[...].astype(BF16), wpp_ref[...])
        x2 = x1_ref[...] + bufs[slot][...].reshape(tile, D_MODEL)
        hp = _rms(x2, gp_ref[...]).astype(BF16)
        gate = jax.nn.sigmoid(_dot(hp, wpg_ref[...]))
        out_ref[...] = _rms(x2 + ple * gate, gf_ref[...])

        @pl.when(i == n - 1)
        def _():
            wait_tile(1 - slot)

    for slot in (0, 1):
        pl.when(i % 2 == slot)(functools.partial(step, slot))


def _tail(meta, ys, x1, p2d, gp, wpg, wpp, gf, tile):
    T = x1.shape[0]
    n = T // tile
    tok = lambda w: pl.BlockSpec((tile, w), lambda i: (i, 0))
    full = lambda a: pl.BlockSpec(a.shape, lambda i: (0,) * a.ndim)
    return pl.pallas_call(
        functools.partial(_tail_kernel, n=n),
        out_shape=jax.ShapeDtypeStruct((T, D_MODEL), F32),
        grid_spec=pltpu.PrefetchScalarGridSpec(
            num_scalar_prefetch=0, grid=(n,),
            in_specs=[pl.BlockSpec((SUBLANES, tile), lambda i: (0, i), memory_space=pltpu.SMEM),
                      pl.BlockSpec((SUBLANES, tile), lambda i: (0, jnp.minimum(i + 1, n - 1)),
                                   memory_space=pltpu.SMEM),
                      pl.BlockSpec(memory_space=pl.ANY), tok(D_MODEL), tok(PLE_DIM),
                      full(gp), full(wpg), full(wpp), full(gf)],
            out_specs=tok(D_MODEL),
            scratch_shapes=[pltpu.VMEM((tile,) + ROW, F32), pltpu.VMEM((tile,) + ROW, F32),
                            pltpu.SemaphoreType.DMA((2,))]),
        compiler_params=_params("arbitrary"),
        name="combine_tail",
    )(meta, meta, ys, x1, p2d, gp, wpg, wpp, gf)


def _class_experts():
    lo, hi = [], []
    for g in range(N_GROUPS):
        for a in range(EXPERTS_PER_GROUP):
            for b in range(a + 1, EXPERTS_PER_GROUP):
                lo.append(g * EXPERTS_PER_GROUP + a)
                hi.append(g * EXPERTS_PER_GROUP + b)
    return np.asarray(lo, np.int32), np.asarray(hi, np.int32)


def _sorted_layout(counts, n_tiles, tile):
    cnt = counts[0, :N_CLASSES].astype(I32)
    padded = (cnt + tile - 1) // tile * tile
    ends = jnp.cumsum(padded)
    starts = ends - padded
    tile_start = jnp.arange(n_tiles, dtype=I32) * tile
    tile_class = jnp.minimum(jnp.sum((ends[None, :] <= tile_start[:, None]).astype(I32), axis=1), N_CLASSES - 1)
    onehot = (tile_class[:, None] == jnp.arange(N_CLASSES, dtype=I32)[None, :]).astype(I32)
    pick = lambda table: jnp.sum(onehot * table[None, :], axis=1).astype(I32)
    tile_rows = jnp.clip(pick(cnt) - (tile_start - pick(starts)), 0, tile)
    tile_rows = jnp.where(tile_start < ends[-1], tile_rows, 0).astype(I32)
    class_lo, class_hi = _class_experts()
    partial = jnp.where(padded > cnt, ends // tile - 1, -1)
    tail = ends[-1] // tile + jnp.arange(N_CLASSES, dtype=I32)
    fill_tiles = jnp.concatenate([partial, jnp.where(tail < n_tiles, tail, -1)]).astype(I32)
    starts_padded = jnp.zeros((LANES,), I32).at[:N_CLASSES].set(starts)
    return starts_padded, pick(jnp.asarray(class_lo)), pick(jnp.asarray(class_hi)), tile_rows, fill_tiles


def _mix_and_route(x, w, counts_in):
    B, S, _ = x.shape
    T = B * S
    x2d = x.reshape(T, D_MODEL)
    z = _inproj(x2d, w["g_mix"], w["w_in"], tile=512)
    z3 = z.reshape(B, S, IN_WIDTH)
    yf = _fourier(z3, w["w_fourier"])
    o = _hgrn(z3, w["lb"], w["g_o"], heads=4 if S <= 2048 else 2)
    return _outproj(x2d, yf.reshape(T, FOURIER_WIDTH), o.reshape(T, HGRN_WIDTH),
                    w["w_out_a"], w["w_out_b"], w["g_ffn"], w["w_router"], counts_in, tile=ROUTE_TILE)


def kernel(x_prompt, x_sample, p_prompt, p_sample, norm_mix, w_in, w_fourier, lb_logits, norm_o, w_out,
           norm_ffn, w_route_group, w_route_expert, w_exp_gate, w_exp_up, w_exp_down, norm_ple,
           w_ple_gate, w_ple_proj, norm_final):
    assert w_in.shape[0] == 1, "single-layer trunk"
    lb_all = jnp.cumsum(jax.nn.softmax(lb_logits.astype(F32), axis=0), axis=0)
    router = jnp.concatenate(
        [w_route_expert[0], w_route_group[0],
         jnp.zeros((D_MODEL, LANES - N_EXPERTS - N_GROUPS), F32)], axis=1)
    w = {
        "g_mix": norm_mix[0][None, :],
        "w_in": _scale_query_columns(w_in[0]).astype(BF16),
        "w_fourier": w_fourier[0],
        "lb": lb_all[0],
        "g_o": norm_o[0][None, :],
        "w_out_a": w_out[0, :FOURIER_WIDTH].astype(BF16),
        "w_out_b": w_out[0, FOURIER_WIDTH:].astype(BF16),
        "g_ffn": norm_ffn[0][None, :],
        "w_router": router.astype(BF16),
    }
    xs_in = (x_prompt, x_sample)
    ps_in = (p_prompt[0], p_sample[0])

    counts = jnp.zeros((SUBLANES, LANES), F32)
    routed = []
    for x in xs_in:
        x1, slab, meta, counts = _mix_and_route(x, w, counts)
        routed.append((x1, slab, meta))

    total = sum(x.shape[0] * x.shape[1] for x in xs_in)
    n_tiles = total // EXPERT_TILE + N_CLASSES
    starts, tile_elo, tile_ehi, tile_rows, fill_tiles = _sorted_layout(counts, n_tiles, EXPERT_TILE)

    positions = [_positions(starts, meta) for _, _, meta in routed]
    xs = _dispatch(fill_tiles, positions, [r[1] for r in routed], n_tiles * EXPERT_TILE, tile=ROUTE_TILE)
    ys = _experts(tile_elo, tile_ehi, tile_rows, xs, w_exp_gate[0].astype(BF16), w_exp_up[0].astype(BF16),
                  w_exp_down[0].astype(BF16), tile=EXPERT_TILE)

    outs = []
    for (x1, _, _), pos, x, p in zip(routed, positions, xs_in, ps_in):
        out = _tail(pos, ys, x1, p.reshape(-1, PLE_DIM), norm_ple[0][None, :],
                    w_ple_gate[0].astype(BF16), w_ple_proj[0].astype(BF16), norm_final[None, :],
                    tile=ROUTE_TILE)
        outs.append(out.reshape(x.shape))
    return tuple(outs)
```

```python
import functools
import math

import numpy as np
import jax
import jax.numpy as jnp
from jax import lax
from jax.experimental import pallas as pl
from jax.experimental.pallas import tpu as pltpu

F32 = jnp.float32
BF16 = jnp.bfloat16
I32 = jnp.int32

D_MODEL = 1024
FOURIER_WIDTH = 512
FOURIER_GROUPS = 4
GROUP_DIM = 128
HGRN_WIDTH = 512
HEAD_DIM = 128
HGRN_HEADS = 4
CHUNK = 64
IN_WIDTH = FOURIER_WIDTH + 5 * HGRN_WIDTH
N_GROUPS = 4
EXPERTS_PER_GROUP = 8
N_EXPERTS = 32
PAIRS_PER_GROUP = EXPERTS_PER_GROUP * (EXPERTS_PER_GROUP - 1) // 2
N_CLASSES = N_GROUPS * PAIRS_PER_GROUP
D_EXPERT = 512
PLE_DIM = 256
EPS = 1e-6
LANES = 128
SUBLANES = 8
VMEM_LIMIT = 56 * 1024 * 1024
ROUTE_TILE = 512
EXPERT_TILE = 256
HGRN_STEPS_PER_TRIP = 8
FOURIER_RESIDENT_ROWS = 1024
FOURIER_TABLE_ROWS = 512
ROW = (SUBLANES, LANES)
SLAB = (2 * SUBLANES, LANES)
SLAB_DTYPE = BF16
SLAB_WIDTH = 2 * D_MODEL
SLAB_ZEROS = (0,) * len(SLAB)
META_CLASS, META_RANK = 0, 1

assert N_CLASSES <= LANES


def _params(*sem):
    return pltpu.CompilerParams(dimension_semantics=sem, vmem_limit_bytes=VMEM_LIMIT)


def _dot(a, b):
    return jnp.dot(a, b, preferred_element_type=F32)


def _dot_nt(a, b):
    return lax.dot_general(a, b, (((1,), (1,)), ((), ())), preferred_element_type=F32)


def _dot_tn(a, b):
    return lax.dot_general(a, b, (((0,), (0,)), ((), ())), preferred_element_type=F32)


def _rms(x, g):
    return x * lax.rsqrt(jnp.mean(x * x, axis=-1, keepdims=True) + EPS) * g


def _inproj_kernel(x_ref, g_ref, w_ref, z_ref):
    h = _rms(x_ref[...], g_ref[...]).astype(BF16)
    for j in range(0, IN_WIDTH, D_MODEL):
        z_ref[:, j:j + D_MODEL] = _dot(h, w_ref[:, j:j + D_MODEL]).astype(z_ref.dtype)


def _inproj(x2d, g, w_bf, tile):
    T = x2d.shape[0]
    return pl.pallas_call(
        _inproj_kernel,
        out_shape=jax.ShapeDtypeStruct((T, IN_WIDTH), BF16),
        grid=(T // tile,),
        in_specs=[
            pl.BlockSpec((tile, D_MODEL), lambda i: (i, 0)),
            pl.BlockSpec((1, D_MODEL), lambda i: (0, 0)),
            pl.BlockSpec((D_MODEL, IN_WIDTH), lambda i: (0, 0)),
        ],
        out_specs=pl.BlockSpec((tile, IN_WIDTH), lambda i: (i, 0)),
        compiler_params=_params("parallel"),
        name="inproj",
    )(x2d, g, w_bf)


def _fourier_fold_kernel(cc_ref, sc_ref, w_ref, m_ref, *, scale):
    for g in range(FOURIER_GROUPS):
        w = w_ref[g]
        m_ref[g, :GROUP_DIM, :] = (jnp.dot(cc_ref[...], w, preferred_element_type=F32,
                                           precision=lax.Precision.HIGHEST) * scale).astype(m_ref.dtype)
        m_ref[g, GROUP_DIM:, :] = (jnp.dot(sc_ref[...], w, preferred_element_type=F32,
                                           precision=lax.Precision.HIGHEST) * (-scale)).astype(m_ref.dtype)


def _fourier_fold(w_fourier, seq):
    cc, sc = _dft_tables(GROUP_DIM, GROUP_DIM, 1, 0, GROUP_DIM, F32)
    scale = 1.0 / math.sqrt(seq * GROUP_DIM)
    return pl.pallas_call(
        functools.partial(_fourier_fold_kernel, scale=scale),
        out_shape=jax.ShapeDtypeStruct((FOURIER_GROUPS, 2 * GROUP_DIM, GROUP_DIM), BF16),
        name="fourier_fold",
    )(cc, sc, w_fourier)


def _dft_tables(n_rows, n_cols, row_mult, row_off, period, dtype):
    r = int(round(math.sqrt(n_rows)))
    while n_rows % r:
        r -= 1
    m = jnp.arange(n_cols, dtype=I32)[None, :]
    k_lo = jnp.arange(r, dtype=I32)[:, None] * row_mult + row_off
    k_hi = jnp.arange(n_rows // r, dtype=I32)[:, None] * (r * row_mult)
    step = 2.0 * math.pi / period
    a_lo = (((k_lo % period) * m) % period).astype(F32) * step
    a_hi = (((k_hi % period) * m) % period).astype(F32) * step
    c_lo, s_lo = jnp.cos(a_lo)[None], jnp.sin(a_lo)[None]
    c_hi, s_hi = jnp.cos(a_hi)[:, None], jnp.sin(a_hi)[:, None]
    cos = (c_hi * c_lo - s_hi * s_lo).reshape(n_rows, n_cols)
    sin = (s_hi * c_lo + c_hi * s_lo).reshape(n_rows, n_cols)
    return cos.astype(dtype), sin.astype(dtype)


def _fourier_kernel(ulo_ref, uhi_ref, ce_ref, se_ref, co_ref, so_ref, m_ref, y_ref, sum_ref, dif_ref, stage_ref):
    tk = ce_ref.shape[0]

    @pl.when(pl.program_id(1) == 0)
    def _():
        lo = ulo_ref[0].astype(F32)
        hi = uhi_ref[0].astype(F32)
        sum_ref[...] = (lo + hi).astype(BF16)
        dif_ref[...] = (lo - hi).astype(BF16)

    def mix(cos_ref, sin_ref, x, parity):
        a = _dot(cos_ref[...], x).astype(BF16)
        b = _dot(sin_ref[...], x).astype(BF16)
        for g in range(FOURIER_GROUPS):
            sl = slice(g * GROUP_DIM, (g + 1) * GROUP_DIM)
            stage_ref[g, pl.ds(parity, tk, stride=2), :] = _dot(
                jnp.concatenate([a[:, sl], b[:, sl]], axis=1), m_ref[g])

    mix(ce_ref, se_ref, sum_ref[...], 0)
    mix(co_ref, so_ref, dif_ref[...], 1)
    for g in range(FOURIER_GROUPS):
        y_ref[0, :, g * GROUP_DIM:(g + 1) * GROUP_DIM] = stage_ref[g].astype(y_ref.dtype)


def _fourier(z3, w_fourier):
    B, S, _ = z3.shape
    M = S // 2
    tk = M if M <= FOURIER_RESIDENT_ROWS else FOURIER_TABLE_ROWS
    ce, se = _dft_tables(M, M, 1, 0, M, BF16)
    co, so = _dft_tables(M, M, 2, 1, S, BF16)
    m = _fourier_fold(w_fourier, S)
    table = pl.BlockSpec((tk, M), lambda b, k: (k, 0))
    return pl.pallas_call(
        _fourier_kernel,
        out_shape=jax.ShapeDtypeStruct((B, S, FOURIER_WIDTH), BF16),
        grid=(B, M // tk),
        in_specs=[
            pl.BlockSpec((1, M, FOURIER_WIDTH), lambda b, k: (b, 0, 0)),
            pl.BlockSpec((1, M, FOURIER_WIDTH), lambda b, k: (b, 1, 0)),
            table, table, table, table,
            pl.BlockSpec((FOURIER_GROUPS, 2 * GROUP_DIM, GROUP_DIM), lambda b, k: (0, 0, 0)),
        ],
        out_specs=pl.BlockSpec((1, 2 * tk, FOURIER_WIDTH), lambda b, k: (b, k, 0)),
        scratch_shapes=[pltpu.VMEM((M, FOURIER_WIDTH), BF16), pltpu.VMEM((M, FOURIER_WIDTH), BF16),
                        pltpu.VMEM((FOURIER_GROUPS, 2 * tk, GROUP_DIM), F32)],
        compiler_params=_params("parallel", "arbitrary"),
        name="fourier",
    )(z3, z3, ce, se, co, so, m)


def _scale_query_columns(w_in):
    col = jnp.arange(IN_WIDTH)
    is_q = (col >= FOURIER_WIDTH) & (col < FOURIER_WIDTH + HGRN_WIDTH)
    return w_in * jnp.where(is_q, HEAD_DIM ** -0.5, 1.0).astype(w_in.dtype)[None, :]


def _split3(x):
    hi = x.astype(BF16)
    r1 = x - hi.astype(F32)
    mid = r1.astype(BF16)
    lo = (r1 - mid.astype(F32)).astype(BF16)
    return hi, mid, lo


def _hgrn_kernel(q_ref, v_ref, ff_ref, fb_ref, og_ref, lb_ref, go_ref, o_ref,
                 acc_ref, st_ref, qd_ref, ki_ref, kd_ref, dec_ref, *, seq, heads):
    n_chunks = seq // CHUNK
    half = n_chunks // 2
    assert HGRN_STEPS_PER_TRIP % 2 == 0 and half % HGRN_STEPS_PER_TRIP == 0
    row = lax.broadcasted_iota(I32, (CHUNK, CHUNK), 0)
    col = lax.broadcasted_iota(I32, (CHUNK, CHUNK), 1)
    masks = (row >= col, row <= col)
    tris = tuple(jnp.where(m, 1.0, 0.0).astype(BF16) for m in masks)
    edges = (CHUNK - 1, 0)
    f_refs = (ff_ref, fb_ref)

    st_ref[...] = jnp.zeros_like(st_ref)

    def chunk_rows(i, d):
        c = i if d == 0 else n_chunks - 1 - i
        return pl.ds(pl.multiple_of(c * CHUNK, CHUNK), CHUNK)

    def prepare(i, d, slot):
        rows = chunk_rows(jnp.minimum(i, n_chunks - 1), d)
        q = q_ref[0, rows, :].astype(F32)
        fr = f_refs[d][0, rows, :].astype(F32)
        lb = lb_ref[d:d + 1, :]
        f = lb + (1.0 - lb) * jax.nn.sigmoid(fr)
        k = 1.0 - f
        hi, mid, lo = _split3(jnp.log(f))
        b = _dot(tris[d], hi) + _dot(tris[d], mid) + _dot(tris[d], lo)
        b_edge = b[edges[d]:edges[d] + 1, :]
        qd_ref[d, slot] = (q * jnp.exp(b)).astype(BF16)
        ki_ref[d, slot] = (k * jnp.exp(-b)).astype(BF16)
        kd_ref[d, slot] = (k * jnp.exp(b_edge - b)).astype(BF16)
        dec_ref[d, slot] = jnp.broadcast_to(jnp.exp(b_edge), dec_ref.shape[2:])

    def advance(i, d, slot):
        rows = chunk_rows(i, d)
        v = v_ref[0, rows, :]
        q_dec, k_inv, k_dec = qd_ref[d, slot], ki_ref[d, slot], kd_ref[d, slot]
        dec = dec_ref[d, slot, 0:1, :]
        outs = []
        for h in range(heads):
            sl = slice(h * HEAD_DIM, (h + 1) * HEAD_DIM)
            st = st_ref[d, h]
            scores = jnp.where(masks[d], _dot_nt(q_dec[:, sl], k_inv[:, sl]), 0.0).astype(BF16)
            outs.append(_dot(scores, v[:, sl]) + _dot_nt(q_dec[:, sl], st.astype(BF16)))
            st_ref[d, h] = st * dec[:, sl] + _dot_tn(v[:, sl], k_dec[:, sl])
        return rows, outs

    def first_touch(rows, outs):
        for h in range(heads):
            acc_ref[rows, h * HEAD_DIM:(h + 1) * HEAD_DIM] = outs[h]

    def finish(rows, outs):
        og = og_ref[0, rows, :].astype(F32)
        for h in range(heads):
            sl = slice(h * HEAD_DIM, (h + 1) * HEAD_DIM)
            o = outs[h] + acc_ref[rows, sl]
            gate = og[:, sl] * jax.nn.sigmoid(og[:, sl])
            o_ref[0, rows, sl] = (_rms(o, go_ref[...]) * gate).astype(o_ref.dtype)

    def steps(sink):
        def body(j, carry):
            for u in range(HGRN_STEPS_PER_TRIP):
                i, slot = HGRN_STEPS_PER_TRIP * j + u, u % 2
                for d in (0, 1):
                    prepare(i + 1, d, 1 - slot)
                for d in (0, 1):
                    sink(*advance(i, d, slot))
            return carry
        return body

    for d in (0, 1):
        prepare(0, d, 0)
    lax.fori_loop(0, half // HGRN_STEPS_PER_TRIP, steps(first_touch), 0)
    lax.fori_loop(half // HGRN_STEPS_PER_TRIP, n_chunks // HGRN_STEPS_PER_TRIP, steps(finish), 0)


def _hgrn(z3, lb, g_o, heads):
    B, S, _ = z3.shape
    assert (S // CHUNK) % 2 == 0 and HGRN_HEADS % heads == 0
    W = heads * HEAD_DIM
    nblk = HGRN_WIDTH // W
    base = FOURIER_WIDTH // W

    def zspec(j):
        return pl.BlockSpec((1, S, W), lambda b, h, j=j: (b, 0, base + j * nblk + h))

    return pl.pallas_call(
        functools.partial(_hgrn_kernel, seq=S, heads=heads),
        out_shape=jax.ShapeDtypeStruct((B, S, HGRN_WIDTH), BF16),
        grid=(B, nblk),
        in_specs=[zspec(0), zspec(1), zspec(2), zspec(3), zspec(4),
                  pl.BlockSpec((2, W), lambda b, h: (0, h)),
                  pl.BlockSpec((1, HEAD_DIM), lambda b, h: (0, 0))],
        out_specs=pl.BlockSpec((1, S, W), lambda b, h: (b, 0, h)),
        scratch_shapes=[pltpu.VMEM((S, W), F32),
                        pltpu.VMEM((2, heads, HEAD_DIM, HEAD_DIM), F32),
                        pltpu.VMEM((2, 2, CHUNK, W), BF16),
                        pltpu.VMEM((2, 2, CHUNK, W), BF16),
                        pltpu.VMEM((2, 2, CHUNK, W), BF16),
                        pltpu.VMEM((2, 2, SUBLANES, W), F32)],
        compiler_params=_params("parallel", "arbitrary"),
        name="hgrn",
    )(z3, z3, z3, z3, z3, lb, g_o)


def _outproj_kernel(x_ref, yf_ref, o_ref, wa_ref, wb_ref, g_ref, wr_ref, tri_ref, cnt_in_ref,
                    x1_ref, slab_ref, meta_ref, counts_ref, cnt_ref):
    @pl.when(pl.program_id(0) == 0)
    def _():
        cnt_ref[...] = cnt_in_ref[...]

    tile = x_ref.shape[0]
    x1 = x_ref[...] + _dot(yf_ref[...], wa_ref[...]) + _dot(o_ref[...], wb_ref[...])
    x1_ref[...] = x1
    hb = _rms(x1, g_ref[...]).astype(BF16)
    logits = _dot(hb, wr_ref[...])
    lane = lax.broadcasted_iota(I32, logits.shape, 1)
    neg = -jnp.inf

    lane_f = lane.astype(F32)

    def argmax_lowest(x, m):
        return jnp.min(jnp.where(x == m, lane_f, float(LANES)), axis=1, keepdims=True).astype(I32)

    lg = jnp.where((lane >= N_EXPERTS) & (lane < N_EXPERTS + N_GROUPS), logits, neg)
    mg = jnp.max(lg, axis=1, keepdims=True)
    p_top = 1.0 / jnp.sum(jnp.exp(lg - mg), axis=1, keepdims=True)
    g_idx = argmax_lowest(lg, mg) - N_EXPERTS
    base = g_idx * EXPERTS_PER_GROUP
    le = jnp.where((lane >= base) & (lane < base + EXPERTS_PER_GROUP), logits, neg)
    v1 = jnp.max(le, axis=1, keepdims=True)
    i1 = argmax_lowest(le, v1)
    le2 = jnp.where(lane == i1, neg, le)
    v2 = jnp.max(le2, axis=1, keepdims=True)
    i2 = argmax_lowest(le2, v2)
    e2 = jnp.exp(v2 - v1)
    w1 = p_top / (1.0 + e2)
    w2 = w1 * e2

    a = jnp.minimum(i1, i2) - base
    b = jnp.maximum(i1, i2) - base
    pair = lax.shift_right_logical(a * (2 * EXPERTS_PER_GROUP - 1 - a), 1) + (b - a - 1)
    cls = g_idx * PAIRS_PER_GROUP + pair
    first_is_lo = i1 < i2
    w_lo = jnp.where(first_is_lo, w1, w2)
    w_hi = jnp.where(first_is_lo, w2, w1)

    onehot = jnp.where(lane == cls, 1.0, 0.0)
    before = _dot(tri_ref[...], onehot.astype(BF16))
    carry = cnt_ref[0:1, :]
    rank = jnp.sum(onehot * (carry + before), axis=1, keepdims=True)
    new_counts = jnp.broadcast_to(carry + jnp.sum(onehot, axis=0, keepdims=True), cnt_ref.shape)
    cnt_ref[...] = new_counts
    counts_ref[...] = new_counts

    def bf16_pair(w):
        w = jnp.broadcast_to(w, (tile, LANES))
        head = w.astype(BF16)
        return [head, (w - head.astype(F32)).astype(BF16)]

    slab = jnp.concatenate([hb] + bf16_pair(w_lo) + bf16_pair(w_hi)
                           + [jnp.zeros((tile, SLAB_WIDTH - D_MODEL - 4 * LANES), BF16)], axis=1)
    slab_ref[...] = slab.reshape((tile,) + SLAB)

    meta = jnp.where(lane == META_CLASS, cls, jnp.where(lane == META_RANK, rank.astype(I32), 0))
    meta_ref[...] = meta.T[0:SUBLANES, :]


def _outproj(x2d, yf2d, o2d, wa, wb, g, wr, counts_in, tile):
    T = x2d.shape[0]
    tri = jnp.tril(jnp.ones((tile, tile), F32), -1).astype(BF16)
    tok = lambda w: pl.BlockSpec((tile, w), lambda i: (i, 0))
    full = lambda a: pl.BlockSpec(a.shape, lambda i: (0,) * a.ndim)
    return pl.pallas_call(
        _outproj_kernel,
        out_shape=(jax.ShapeDtypeStruct((T, D_MODEL), F32),
                   jax.ShapeDtypeStruct((T,) + SLAB, SLAB_DTYPE),
                   jax.ShapeDtypeStruct((SUBLANES, T), I32),
                   jax.ShapeDtypeStruct((SUBLANES, LANES), F32)),
        grid=(T // tile,),
        in_specs=[tok(D_MODEL), tok(FOURIER_WIDTH), tok(HGRN_WIDTH), full(wa), full(wb), full(g), full(wr),
                  full(tri), full(counts_in)],
        out_specs=(tok(D_MODEL), pl.BlockSpec((tile,) + SLAB, lambda i: (i,) + SLAB_ZEROS),
                   pl.BlockSpec((SUBLANES, tile), lambda i: (0, i)),
                   pl.BlockSpec((SUBLANES, LANES), lambda i: (0, 0))),
        scratch_shapes=[pltpu.VMEM((SUBLANES, LANES), F32)],
        compiler_params=_params("arbitrary"),
        name="outproj_router",
    )(x2d, yf2d, o2d, wa, wb, g, wr, tri, counts_in)


def _positions_kernel(starts_ref, meta_ref, pos_ref):
    cls = meta_ref[META_CLASS:META_CLASS + 1, :]
    pos = meta_ref[META_RANK:META_RANK + 1, :]
    for c in range(N_CLASSES):
        pos = pos + jnp.where(cls == c, starts_ref[c], 0)
    pos_ref[...] = jnp.broadcast_to(pos, pos_ref.shape)


def _positions(starts, meta, tile=4096):
    T = meta.shape[1]
    tile = min(tile, T)
    block = pl.BlockSpec((SUBLANES, tile), lambda i, s: (0, i))
    return pl.pallas_call(
        _positions_kernel,
        out_shape=jax.ShapeDtypeStruct(meta.shape, I32),
        grid_spec=pltpu.PrefetchScalarGridSpec(num_scalar_prefetch=1, grid=(T // tile,),
                                               in_specs=[block], out_specs=block),
        compiler_params=_params("parallel"),
        name="moe_positions",
    )(starts, meta)


def _sorted_row(pos_ref, t):
    return pos_ref[0, t]


ROW_COPY_UNROLL = 8


def _start_row_copies(n, make):
    assert n % ROW_COPY_UNROLL == 0

    def body(j, carry):
        for u in range(ROW_COPY_UNROLL):
            make(j * ROW_COPY_UNROLL + u).start()
        return carry
    lax.fori_loop(0, n // ROW_COPY_UNROLL, body, 0)


def _dispatch_kernel(fill_ref, *refs, bounds):
    n_trunks = len(bounds) - 1
    meta_refs, slab_refs = refs[0:2 * n_trunks:2], refs[1:2 * n_trunks:2]
    xs_hbm, zero_ref, sem = refs[2 * n_trunks:]
    i = pl.program_id(0)
    ftile = zero_ref.shape[0]

    @pl.when(i == 0)
    def _():
        zero_ref[...] = jnp.zeros_like(zero_ref)
        fill = lambda j: pltpu.make_async_copy(
            zero_ref, xs_hbm.at[pl.ds(pl.multiple_of(fill_ref[j] * ftile, ftile), ftile)], sem.at[1])

        def start(j, carry):
            @pl.when(fill_ref[j] >= 0)
            def _():
                fill(j).start()
            return carry

        def wait(j, carry):
            @pl.when(fill_ref[j] >= 0)
            def _():
                fill(j).wait()
            return carry

        lax.fori_loop(0, fill_ref.shape[0], start, 0)
        lax.fori_loop(0, fill_ref.shape[0], wait, 0)

    for k in range(n_trunks):
        @pl.when((i >= bounds[k]) & (i < bounds[k + 1]))
        def _(meta_ref=meta_refs[k], slab_ref=slab_refs[k]):
            tile = slab_ref.shape[0]
            _start_row_copies(tile, lambda t: pltpu.make_async_copy(
                slab_ref.at[t], xs_hbm.at[_sorted_row(meta_ref, t)], sem.at[0]))
            pltpu.make_async_copy(slab_ref, xs_hbm.at[pl.ds(0, tile)], sem.at[0]).wait()


def _dispatch(fill_tiles, metas, slabs, n_rows, tile):
    steps = [slab.shape[0] // tile for slab in slabs]
    bounds = tuple(int(b) for b in np.cumsum([0] + steps))
    in_specs, args = [], []
    for k, (meta, slab) in enumerate(zip(metas, slabs)):
        local = lambda i, k=k: jnp.clip(i - bounds[k], 0, steps[k] - 1)
        in_specs.append(pl.BlockSpec((SUBLANES, tile), lambda i, f, local=local: (0, local(i)),
                                     memory_space=pltpu.SMEM))
        in_specs.append(pl.BlockSpec((tile,) + SLAB, lambda i, f, local=local: (local(i),) + SLAB_ZEROS))
        args += [meta, slab]
    return pl.pallas_call(
        functools.partial(_dispatch_kernel, bounds=bounds),
        out_shape=jax.ShapeDtypeStruct((n_rows,) + SLAB, SLAB_DTYPE),
        grid_spec=pltpu.PrefetchScalarGridSpec(
            num_scalar_prefetch=1, grid=(bounds[-1],),
            in_specs=in_specs,
            out_specs=pl.BlockSpec(memory_space=pl.ANY),
            scratch_shapes=[pltpu.VMEM((EXPERT_TILE,) + SLAB, SLAB_DTYPE), pltpu.SemaphoreType.DMA((2,))]),
        compiler_params=_params("arbitrary"),
        name="moe_dispatch",
    )(fill_tiles, *args)


def _expert_kernel(elo_ref, ehi_ref, rows_ref, xs_ref, wg0_ref, wu0_ref, wd0_ref, wg1_ref, wu1_ref, wd1_ref,
                   ys_ref):
    i = pl.program_id(0)
    tile = xs_ref.shape[0]
    valid = rows_ref[i]

    @pl.when(valid > 0)
    def _():
        slab = xs_ref[...].reshape(tile, SLAB_WIDTH)
        x = slab[:, :D_MODEL]
        chunk = lambda j: slab[:, D_MODEL + j * LANES:D_MODEL + (j + 1) * LANES].astype(F32)
        y = None
        for k, (wg_ref, wu_ref, wd_ref) in enumerate(((wg0_ref, wu0_ref, wd0_ref), (wg1_ref, wu1_ref, wd1_ref))):
            a = _dot(x, wg_ref[0])
            hid = (a * jax.nn.sigmoid(a) * _dot(x, wu_ref[0])).astype(BF16)
            weight = jnp.tile(chunk(2 * k) + chunk(2 * k + 1), (1, D_MODEL // LANES))
            part = weight * _dot(hid, wd_ref[0])
            y = part if y is None else y + part
        ys_ref[...] = y.reshape(ys_ref.shape)

    @pl.when(valid == 0)
    def _():
        ys_ref[...] = jnp.zeros_like(ys_ref)


def _experts(tile_elo, tile_ehi, tile_rows, xs, wg, wu, wd, tile):
    n_tiles = xs.shape[0] // tile
    lo = lambda a: pl.BlockSpec((1,) + a.shape[1:], lambda i, elo, ehi, rows: (elo[i], 0, 0))
    hi = lambda a: pl.BlockSpec((1,) + a.shape[1:], lambda i, elo, ehi, rows: (ehi[i], 0, 0))
    return pl.pallas_call(
        _expert_kernel,
        out_shape=jax.ShapeDtypeStruct((xs.shape[0],) + ROW, F32),
        grid_spec=pltpu.PrefetchScalarGridSpec(
            num_scalar_prefetch=3, grid=(n_tiles,),
            in_specs=[pl.BlockSpec((tile,) + SLAB, lambda i, elo, ehi, rows: (i,) + SLAB_ZEROS),
                      lo(wg), lo(wu), lo(wd), hi(wg), hi(wu), hi(wd)],
            out_specs=pl.BlockSpec((tile,) + ROW, lambda i, elo, ehi, rows: (i, 0, 0))),
        compiler_params=_params("arbitrary"),
        name="moe_experts",
    )(tile_elo, tile_ehi, tile_rows, xs, wg, wu, wd, wg, wu, wd)


def _tail_kernel(meta_ref, meta_next_ref, ys_hbm, x1_ref, p_ref, gp_ref, wpg_ref, wpp_ref, gf_ref,
                 out_ref, buf_ref, buf2_ref, sem, *, n):
    i = pl.program_id(0)
    tile = x1_ref.shape[0]
    bufs = (buf_ref, buf2_ref)

    def row_copy(m_ref, slot, t):
        return pltpu.make_async_copy(ys_hbm.at[_sorted_row(m_ref, t)], bufs[slot].at[t], sem.at[slot])

    def wait_tile(slot):
        pltpu.make_async_copy(ys_hbm.at[pl.ds(0, tile)], bufs[slot], sem.at[slot]).wait()

    @pl.when(i == 0)
    def _():
        _start_row_copies(tile, lambda t: row_copy(meta_ref, 0, t))

    def step(slot):
        wait_tile(slot)
        for t in range(tile):
            row_copy(meta_next_ref, 1 - slot, t).start()
        ple = _dot(p_ref[...].astype(BF16), wpp_ref[...])
        x2 = x1_ref[...] + bufs[slot][...].reshape(tile, D_MODEL)
        hp = _rms(x2, gp_ref[...]).astype(BF16)
        gate = jax.nn.sigmoid(_dot(hp, wpg_ref[...]))
        out_ref[...] = _rms(x2 + ple * gate, gf_ref[...])

        @pl.when(i == n - 1)
        def _():
            wait_tile(1 - slot)

    for slot in (0, 1):
        pl.when(i % 2 == slot)(functools.partial(step, slot))


def _tail(meta, ys, x1, p2d, gp, wpg, wpp, gf, tile):
    T = x1.shape[0]
    n = T // tile
    tok = lambda w: pl.BlockSpec((tile, w), lambda i: (i, 0))
    full = lambda a: pl.BlockSpec(a.shape, lambda i: (0,) * a.ndim)
    return pl.pallas_call(
        functools.partial(_tail_kernel, n=n),
        out_shape=jax.ShapeDtypeStruct((T, D_MODEL), F32),
        grid_spec=pltpu.PrefetchScalarGridSpec(
            num_scalar_prefetch=0, grid=(n,),
            in_specs=[pl.BlockSpec((SUBLANES, tile), lambda i: (0, i), memory_space=pltpu.SMEM),
                      pl.BlockSpec((SUBLANES, tile), lambda i: (0, jnp.minimum(i + 1, n - 1)),
                                   memory_space=pltpu.SMEM),
                      pl.BlockSpec(memory_space=pl.ANY), tok(D_MODEL), tok(PLE_DIM),
                      full(gp), full(wpg), full(wpp), full(gf)],
            out_specs=tok(D_MODEL),
            scratch_shapes=[pltpu.VMEM((tile,) + ROW, F32), pltpu.VMEM((tile,) + ROW, F32),
                            pltpu.SemaphoreType.DMA((2,))]),
        compiler_params=_params("arbitrary"),
        name="combine_tail",
    )(meta, meta, ys, x1, p2d, gp, wpg, wpp, gf)


def _class_experts():
    lo, hi = [], []
    for g in range(N_GROUPS):
        for a in range(EXPERTS_PER_GROUP):
            for b in range(a + 1, EXPERTS_PER_GROUP):
                lo.append(g * EXPERTS_PER_GROUP + a)
                hi.append(g * EXPERTS_PER_GROUP + b)
    return np.asarray(lo, np.int32), np.asarray(hi, np.int32)


def _sorted_layout(counts, n_tiles, tile):
    cnt = counts[0, :N_CLASSES].astype(I32)
    padded = (cnt + tile - 1) // tile * tile
    ends = jnp.cumsum(padded)
    starts = ends - padded
    tile_start = jnp.arange(n_tiles, dtype=I32) * tile
    tile_class = jnp.minimum(jnp.sum((ends[None, :] <= tile_start[:, None]).astype(I32), axis=1), N_CLASSES - 1)
    onehot = (tile_class[:, None] == jnp.arange(N_CLASSES, dtype=I32)[None, :]).astype(I32)
    pick = lambda table: jnp.sum(onehot * table[None, :], axis=1).astype(I32)
    tile_rows = jnp.clip(pick(cnt) - (tile_start - pick(starts)), 0, tile)
    tile_rows = jnp.where(tile_start < ends[-1], tile_rows, 0).astype(I32)
    class_lo, class_hi = _class_experts()
    partial = jnp.where(padded > cnt, ends // tile - 1, -1)
    tail = ends[-1] // tile + jnp.arange(N_CLASSES, dtype=I32)
    fill_tiles = jnp.concatenate([partial, jnp.where(tail < n_tiles, tail, -1)]).astype(I32)
    starts_padded = jnp.zeros((LANES,), I32).at[:N_CLASSES].set(starts)
    return starts_padded, pick(jnp.asarray(class_lo)), pick(jnp.asarray(class_hi)), tile_rows, fill_tiles


def _mix_and_route(x, w, counts_in):
    B, S, _ = x.shape
    T = B * S
    x2d = x.reshape(T, D_MODEL)
    z = _inproj(x2d, w["g_mix"], w["w_in"], tile=512)
    z3 = z.reshape(B, S, IN_WIDTH)
    yf = _fourier(z3, w["w_fourier"])
    o = _hgrn(z3, w["lb"], w["g_o"], heads=4 if S <= 2048 else 2)
    return _outproj(x2d, yf.reshape(T, FOURIER_WIDTH), o.reshape(T, HGRN_WIDTH),
                    w["w_out_a"], w["w_out_b"], w["g_ffn"], w["w_router"], counts_in, tile=ROUTE_TILE)


def kernel(x_prompt, x_sample, p_prompt, p_sample, norm_mix, w_in, w_fourier, lb_logits, norm_o, w_out,
           norm_ffn, w_route_group, w_route_expert, w_exp_gate, w_exp_up, w_exp_down, norm_ple,
           w_ple_gate, w_ple_proj, norm_final):
    assert w_in.shape[0] == 1, "single-layer trunk"
    lb_all = jnp.cumsum(jax.nn.softmax(lb_logits.astype(F32), axis=0), axis=0)
    router = jnp.concatenate(
        [w_route_expert[0], w_route_group[0],
         jnp.zeros((D_MODEL, LANES - N_EXPERTS - N_GROUPS), F32)], axis=1)
    w = {
        "g_mix": norm_mix[0][None, :],
        "w_in": _scale_query_columns(w_in[0]).astype(BF16),
        "w_fourier": w_fourier[0],
        "lb": lb_all[0],
        "g_o": norm_o[0][None, :],
        "w_out_a": w_out[0, :FOURIER_WIDTH].astype(BF16),
        "w_out_b": w_out[0, FOURIER_WIDTH:].astype(BF16),
        "g_ffn": norm_ffn[0][None, :],
        "w_router": router.astype(BF16),
    }
    xs_in = (x_prompt, x_sample)
    ps_in = (p_prompt[0], p_sample[0])

    counts = jnp.zeros((SUBLANES, LANES), F32)
    routed = []
    for x in xs_in:
        x1, slab, meta, counts = _mix_and_route(x, w, counts)
        routed.append((x1, slab, meta))

    total = sum(x.shape[0] * x.shape[1] for x in xs_in)
    n_tiles = total // EXPERT_TILE + N_CLASSES
    starts, tile_elo, tile_ehi, tile_rows, fill_tiles = _sorted_layout(counts, n_tiles, EXPERT_TILE)

    positions = [_positions(starts, meta) for _, _, meta in routed]
    xs = _dispatch(fill_tiles, positions, [r[1] for r in routed], n_tiles * EXPERT_TILE, tile=ROUTE_TILE)
    ys = _experts(tile_elo, tile_ehi, tile_rows, xs, w_exp_gate[0].astype(BF16), w_exp_up[0].astype(BF16),
                  w_exp_down[0].astype(BF16), tile=EXPERT_TILE)

    outs = []
    for (x1, _, _), pos, x, p in zip(routed, positions, xs_in, ps_in):
        out = _tail(pos, ys, x1, p.reshape(-1, PLE_DIM), norm_ple[0][None, :],
                    w_ple_gate[0].astype(BF16), w_ple_proj[0].astype(BF16), norm_final[None, :],
                    tile=ROUTE_TILE)
        outs.append(out.reshape(x.shape))
    return tuple(outs)
```

```python
import functools
import math

import numpy as np
import jax
import jax.numpy as jnp
from jax import lax
from jax.experimental import pallas as pl
from jax.experimental.pallas import tpu as pltpu

F32 = jnp.float32
BF16 = jnp.bfloat16
I32 = jnp.int32

D_MODEL = 1024
FOURIER_WIDTH = 512
FOURIER_GROUPS = 4
GROUP_DIM = 128
HGRN_WIDTH = 512
HEAD_DIM = 128
HGRN_HEADS = 4
CHUNK = 64
IN_WIDTH = FOURIER_WIDTH + 5 * HGRN_WIDTH
N_GROUPS = 4
EXPERTS_PER_GROUP = 8
N_EXPERTS = 32
PAIRS_PER_GROUP = EXPERTS_PER_GROUP * (EXPERTS_PER_GROUP - 1) // 2
N_CLASSES = N_GROUPS * PAIRS_PER_GROUP
D_EXPERT = 512
PLE_DIM = 256
EPS = 1e-6
LANES = 128
SUBLANES = 8
VMEM_LIMIT = 56 * 1024 * 1024
ROUTE_TILE = 512
EXPERT_TILE = 256
HGRN_STEPS_PER_TRIP = 16
FOURIER_RESIDENT_ROWS = 1024
FOURIER_TABLE_ROWS = 512
ROW = (SUBLANES, LANES)
SLAB = (2 * SUBLANES, LANES)
SLAB_DTYPE = BF16
SLAB_WIDTH = 2 * D_MODEL
SLAB_ZEROS = (0,) * len(SLAB)
META_CLASS, META_RANK = 0, 1

assert N_CLASSES <= LANES


def _params(*sem):
    return pltpu.CompilerParams(dimension_semantics=sem, vmem_limit_bytes=VMEM_LIMIT)


def _dot(a, b):
    return jnp.dot(a, b, preferred_element_type=F32)


def _dot_nt(a, b):
    return lax.dot_general(a, b, (((1,), (1,)), ((), ())), preferred_element_type=F32)


def _dot_tn(a, b):
    return lax.dot_general(a, b, (((0,), (0,)), ((), ())), preferred_element_type=F32)


def _rms(x, g):
    return x * lax.rsqrt(jnp.mean(x * x, axis=-1, keepdims=True) + EPS) * g


def _inproj_kernel(x_ref, g_ref, w_ref, z_ref):
    h = _rms(x_ref[...], g_ref[...]).astype(BF16)
    for j in range(0, IN_WIDTH, D_MODEL):
        z_ref[:, j:j + D_MODEL] = _dot(h, w_ref[:, j:j + D_MODEL]).astype(z_ref.dtype)


def _inproj(x2d, g, w_bf, tile):
    T = x2d.shape[0]
    return pl.pallas_call(
        _inproj_kernel,
        out_shape=jax.ShapeDtypeStruct((T, IN_WIDTH), BF16),
        grid=(T // tile,),
        in_specs=[
            pl.BlockSpec((tile, D_MODEL), lambda i: (i, 0)),
            pl.BlockSpec((1, D_MODEL), lambda i: (0, 0)),
            pl.BlockSpec((D_MODEL, IN_WIDTH), lambda i: (0, 0)),
        ],
        out_specs=pl.BlockSpec((tile, IN_WIDTH), lambda i: (i, 0)),
        compiler_params=_params("parallel"),
        name="inproj",
    )(x2d, g, w_bf)


def _fourier_fold_kernel(cc_ref, sc_ref, w_ref, m_ref, *, scale):
    for g in range(FOURIER_GROUPS):
        w = w_ref[g]
        m_ref[g, :GROUP_DIM, :] = (jnp.dot(cc_ref[...], w, preferred_element_type=F32,
                                           precision=lax.Precision.HIGHEST) * scale).astype(m_ref.dtype)
        m_ref[g, GROUP_DIM:, :] = (jnp.dot(sc_ref[...], w, preferred_element_type=F32,
                                           precision=lax.Precision.HIGHEST) * (-scale)).astype(m_ref.dtype)


def _fourier_fold(w_fourier, seq):
    cc, sc = _dft_tables(GROUP_DIM, GROUP_DIM, 1, 0, GROUP_DIM, F32)
    scale = 1.0 / math.sqrt(seq * GROUP_DIM)
    return pl.pallas_call(
        functools.partial(_fourier_fold_kernel, scale=scale),
        out_shape=jax.ShapeDtypeStruct((FOURIER_GROUPS, 2 * GROUP_DIM, GROUP_DIM), BF16),
        name="fourier_fold",
    )(cc, sc, w_fourier)


def _dft_tables(n_rows, n_cols, row_mult, row_off, period, dtype):
    r = int(round(math.sqrt(n_rows)))
    while n_rows % r:
        r -= 1
    m = jnp.arange(n_cols, dtype=I32)[None, :]
    k_lo = jnp.arange(r, dtype=I32)[:, None] * row_mult + row_off
    k_hi = jnp.arange(n_rows // r, dtype=I32)[:, None] * (r * row_mult)
    step = 2.0 * math.pi / period
    a_lo = (((k_lo % period) * m) % period).astype(F32) * step
    a_hi = (((k_hi % period) * m) % period).astype(F32) * step
    c_lo, s_lo = jnp.cos(a_lo)[None], jnp.sin(a_lo)[None]
    c_hi, s_hi = jnp.cos(a_hi)[:, None], jnp.sin(a_hi)[:, None]
    cos = (c_hi * c_lo - s_hi * s_lo).reshape(n_rows, n_cols)
    sin = (s_hi * c_lo + c_hi * s_lo).reshape(n_rows, n_cols)
    return cos.astype(dtype), sin.astype(dtype)


def _fourier_kernel(ulo_ref, uhi_ref, ce_ref, se_ref, co_ref, so_ref, m_ref, y_ref, sum_ref, dif_ref, stage_ref):
    tk = ce_ref.shape[0]

    @pl.when(pl.program_id(1) == 0)
    def _():
        lo = ulo_ref[0].astype(F32)
        hi = uhi_ref[0].astype(F32)
        sum_ref[...] = (lo + hi).astype(BF16)
        dif_ref[...] = (lo - hi).astype(BF16)

    def mix(cos_ref, sin_ref, x, parity):
        a = _dot(cos_ref[...], x).astype(BF16)
        b = _dot(sin_ref[...], x).astype(BF16)
        for g in range(FOURIER_GROUPS):
            sl = slice(g * GROUP_DIM, (g + 1) * GROUP_DIM)
            stage_ref[g, pl.ds(parity, tk, stride=2), :] = _dot(
                jnp.concatenate([a[:, sl], b[:, sl]], axis=1), m_ref[g])

    mix(ce_ref, se_ref, sum_ref[...], 0)
    mix(co_ref, so_ref, dif_ref[...], 1)
    for g in range(FOURIER_GROUPS):
        y_ref[0, :, g * GROUP_DIM:(g + 1) * GROUP_DIM] = stage_ref[g].astype(y_ref.dtype)


def _fourier(z3, w_fourier):
    B, S, _ = z3.shape
    M = S // 2
    tk = M if M <= FOURIER_RESIDENT_ROWS else FOURIER_TABLE_ROWS
    ce, se = _dft_tables(M, M, 1, 0, M, BF16)
    co, so = _dft_tables(M, M, 2, 1, S, BF16)
    m = _fourier_fold(w_fourier, S)
    table = pl.BlockSpec((tk, M), lambda b, k: (k, 0))
    return pl.pallas_call(
        _fourier_kernel,
        out_shape=jax.ShapeDtypeStruct((B, S, FOURIER_WIDTH), BF16),
        grid=(B, M // tk),
        in_specs=[
            pl.BlockSpec((1, M, FOURIER_WIDTH), lambda b, k: (b, 0, 0)),
            pl.BlockSpec((1, M, FOURIER_WIDTH), lambda b, k: (b, 1, 0)),
            table, table, table, table,
            pl.BlockSpec((FOURIER_GROUPS, 2 * GROUP_DIM, GROUP_DIM), lambda b, k: (0, 0, 0)),
        ],
        out_specs=pl.BlockSpec((1, 2 * tk, FOURIER_WIDTH), lambda b, k: (b, k, 0)),
        scratch_shapes=[pltpu.VMEM((M, FOURIER_WIDTH), BF16), pltpu.VMEM((M, FOURIER_WIDTH), BF16),
                        pltpu.VMEM((FOURIER_GROUPS, 2 * tk, GROUP_DIM), F32)],
        compiler_params=_params("parallel", "arbitrary"),
        name="fourier",
    )(z3, z3, ce, se, co, so, m)


def _scale_query_columns(w_in):
    col = jnp.arange(IN_WIDTH)
    is_q = (col >= FOURIER_WIDTH) & (col < FOURIER_WIDTH + HGRN_WIDTH)
    return w_in * jnp.where(is_q, HEAD_DIM ** -0.5, 1.0).astype(w_in.dtype)[None, :]


def _split3(x):
    hi = x.astype(BF16)
    r1 = x - hi.astype(F32)
    mid = r1.astype(BF16)
    lo = (r1 - mid.astype(F32)).astype(BF16)
    return hi, mid, lo


def _hgrn_kernel(q_ref, v_ref, ff_ref, fb_ref, og_ref, lb_ref, go_ref, o_ref,
                 acc_ref, st_ref, qd_ref, ki_ref, kd_ref, dec_ref, *, seq, heads):
    n_chunks = seq // CHUNK
    half = n_chunks // 2
    per_trip = min(HGRN_STEPS_PER_TRIP, half)
    assert per_trip % 2 == 0 and half % per_trip == 0
    row = lax.broadcasted_iota(I32, (CHUNK, CHUNK), 0)
    col = lax.broadcasted_iota(I32, (CHUNK, CHUNK), 1)
    masks = (row >= col, row <= col)
    tris = tuple(jnp.where(m, 1.0, 0.0).astype(BF16) for m in masks)
    edges = (CHUNK - 1, 0)
    f_refs = (ff_ref, fb_ref)

    st_ref[...] = jnp.zeros_like(st_ref)

    def chunk_rows(i, d):
        c = i if d == 0 else n_chunks - 1 - i
        return pl.ds(pl.multiple_of(c * CHUNK, CHUNK), CHUNK)

    def prepare(i, d, slot):
        rows = chunk_rows(jnp.minimum(i, n_chunks - 1), d)
        q = q_ref[0, rows, :].astype(F32)
        fr = f_refs[d][0, rows, :].astype(F32)
        lb = lb_ref[d:d + 1, :]
        f = lb + (1.0 - lb) * jax.nn.sigmoid(fr)
        k = 1.0 - f
        hi, mid, lo = _split3(jnp.log(f))
        b = _dot(tris[d], hi) + _dot(tris[d], mid) + _dot(tris[d], lo)
        b_edge = b[edges[d]:edges[d] + 1, :]
        qd_ref[d, slot] = (q * jnp.exp(b)).astype(BF16)
        ki_ref[d, slot] = (k * jnp.exp(-b)).astype(BF16)
        kd_ref[d, slot] = (k * jnp.exp(b_edge - b)).astype(BF16)
        dec_ref[d, slot] = jnp.broadcast_to(jnp.exp(b_edge), dec_ref.shape[2:])

    def advance(i, d, slot):
        rows = chunk_rows(i, d)
        v = v_ref[0, rows, :]
        q_dec, k_inv, k_dec = qd_ref[d, slot], ki_ref[d, slot], kd_ref[d, slot]
        dec = dec_ref[d, slot, 0:1, :]
        outs = []
        for h in range(heads):
            sl = slice(h * HEAD_DIM, (h + 1) * HEAD_DIM)
            st = st_ref[d, h]
            scores = jnp.where(masks[d], _dot_nt(q_dec[:, sl], k_inv[:, sl]), 0.0).astype(BF16)
            outs.append(_dot(scores, v[:, sl]) + _dot_nt(q_dec[:, sl], st.astype(BF16)))
            st_ref[d, h] = st * dec[:, sl] + _dot_tn(v[:, sl], k_dec[:, sl])
        return rows, outs

    def first_touch(rows, outs):
        for h in range(heads):
            acc_ref[rows, h * HEAD_DIM:(h + 1) * HEAD_DIM] = outs[h]

    def finish(rows, outs):
        og = og_ref[0, rows, :].astype(F32)
        for h in range(heads):
            sl = slice(h * HEAD_DIM, (h + 1) * HEAD_DIM)
            o = outs[h] + acc_ref[rows, sl]
            gate = og[:, sl] * jax.nn.sigmoid(og[:, sl])
            o_ref[0, rows, sl] = (_rms(o, go_ref[...]) * gate).astype(o_ref.dtype)

    def steps(sink):
        def body(j, carry):
            for u in range(per_trip):
                i, slot = per_trip * j + u, u % 2
                for d in (0, 1):
                    prepare(i + 1, d, 1 - slot)
                for d in (0, 1):
                    sink(*advance(i, d, slot))
            return carry
        return body

    for d in (0, 1):
        prepare(0, d, 0)
    lax.fori_loop(0, half // per_trip, steps(first_touch), 0)
    lax.fori_loop(half // per_trip, n_chunks // per_trip, steps(finish), 0)


def _hgrn(z3, lb, g_o, heads):
    B, S, _ = z3.shape
    assert (S // CHUNK) % 2 == 0 and HGRN_HEADS % heads == 0
    W = heads * HEAD_DIM
    nblk = HGRN_WIDTH // W
    base = FOURIER_WIDTH // W

    def zspec(j):
        return pl.BlockSpec((1, S, W), lambda b, h, j=j: (b, 0, base + j * nblk + h))

    return pl.pallas_call(
        functools.partial(_hgrn_kernel, seq=S, heads=heads),
        out_shape=jax.ShapeDtypeStruct((B, S, HGRN_WIDTH), BF16),
        grid=(B, nblk),
        in_specs=[zspec(0), zspec(1), zspec(2), zspec(3), zspec(4),
                  pl.BlockSpec((2, W), lambda b, h: (0, h)),
                  pl.BlockSpec((1, HEAD_DIM), lambda b, h: (0, 0))],
        out_specs=pl.BlockSpec((1, S, W), lambda b, h: (b, 0, h)),
        scratch_shapes=[pltpu.VMEM((S, W), F32),
                        pltpu.VMEM((2, heads, HEAD_DIM, HEAD_DIM), F32),
                        pltpu.VMEM((2, 2, CHUNK, W), BF16),
                        pltpu.VMEM((2, 2, CHUNK, W), BF16),
                        pltpu.VMEM((2, 2, CHUNK, W), BF16),
                        pltpu.VMEM((2, 2, SUBLANES, W), F32)],
        compiler_params=_params("parallel", "arbitrary"),
        name="hgrn",
    )(z3, z3, z3, z3, z3, lb, g_o)


def _outproj_kernel(x_ref, yf_ref, o_ref, wa_ref, wb_ref, g_ref, wr_ref, tri_ref, cnt_in_ref,
                    x1_ref, slab_ref, meta_ref, counts_ref, cnt_ref):
    @pl.when(pl.program_id(0) == 0)
    def _():
        cnt_ref[...] = cnt_in_ref[...]

    tile = x_ref.shape[0]
    x1 = x_ref[...] + _dot(yf_ref[...], wa_ref[...]) + _dot(o_ref[...], wb_ref[...])
    x1_ref[...] = x1
    hb = _rms(x1, g_ref[...]).astype(BF16)
    logits = _dot(hb, wr_ref[...])
    lane = lax.broadcasted_iota(I32, logits.shape, 1)
    neg = -jnp.inf

    lane_f = lane.astype(F32)

    def argmax_lowest(x, m):
        return jnp.min(jnp.where(x == m, lane_f, float(LANES)), axis=1, keepdims=True).astype(I32)

    lg = jnp.where((lane >= N_EXPERTS) & (lane < N_EXPERTS + N_GROUPS), logits, neg)
    mg = jnp.max(lg, axis=1, keepdims=True)
    p_top = 1.0 / jnp.sum(jnp.exp(lg - mg), axis=1, keepdims=True)
    g_idx = argmax_lowest(lg, mg) - N_EXPERTS
    base = g_idx * EXPERTS_PER_GROUP
    le = jnp.where((lane >= base) & (lane < base + EXPERTS_PER_GROUP), logits, neg)
    v1 = jnp.max(le, axis=1, keepdims=True)
    i1 = argmax_lowest(le, v1)
    le2 = jnp.where(lane == i1, neg, le)
    v2 = jnp.max(le2, axis=1, keepdims=True)
    i2 = argmax_lowest(le2, v2)
    e2 = jnp.exp(v2 - v1)
    w1 = p_top / (1.0 + e2)
    w2 = w1 * e2

    a = jnp.minimum(i1, i2) - base
    b = jnp.maximum(i1, i2) - base
    pair = lax.shift_right_logical(a * (2 * EXPERTS_PER_GROUP - 1 - a), 1) + (b - a - 1)
    cls = g_idx * PAIRS_PER_GROUP + pair
    first_is_lo = i1 < i2
    w_lo = jnp.where(first_is_lo, w1, w2)
    w_hi = jnp.where(first_is_lo, w2, w1)

    onehot = jnp.where(lane == cls, 1.0, 0.0)
    before = _dot(tri_ref[...], onehot.astype(BF16))
    carry = cnt_ref[0:1, :]
    rank = jnp.sum(onehot * (carry + before), axis=1, keepdims=True)
    new_counts = jnp.broadcast_to(carry + jnp.sum(onehot, axis=0, keepdims=True), cnt_ref.shape)
    cnt_ref[...] = new_counts
    counts_ref[...] = new_counts

    def bf16_pair(w):
        w = jnp.broadcast_to(w, (tile, LANES))
        head = w.astype(BF16)
        return [head, (w - head.astype(F32)).astype(BF16)]

    slab = jnp.concatenate([hb] + bf16_pair(w_lo) + bf16_pair(w_hi)
                           + [jnp.zeros((tile, SLAB_WIDTH - D_MODEL - 4 * LANES), BF16)], axis=1)
    slab_ref[...] = slab.reshape((tile,) + SLAB)

    meta = jnp.where(lane == META_CLASS, cls, jnp.where(lane == META_RANK, rank.astype(I32), 0))
    meta_ref[...] = meta.T[0:SUBLANES, :]


def _outproj(x2d, yf2d, o2d, wa, wb, g, wr, counts_in, tile):
    T = x2d.shape[0]
    tri = jnp.tril(jnp.ones((tile, tile), F32), -1).astype(BF16)
    tok = lambda w: pl.BlockSpec((tile, w), lambda i: (i, 0))
    full = lambda a: pl.BlockSpec(a.shape, lambda i: (0,) * a.ndim)
    return pl.pallas_call(
        _outproj_kernel,
        out_shape=(jax.ShapeDtypeStruct((T, D_MODEL), F32),
                   jax.ShapeDtypeStruct((T,) + SLAB, SLAB_DTYPE),
                   jax.ShapeDtypeStruct((SUBLANES, T), I32),
                   jax.ShapeDtypeStruct((SUBLANES, LANES), F32)),
        grid=(T // tile,),
        in_specs=[tok(D_MODEL), tok(FOURIER_WIDTH), tok(HGRN_WIDTH), full(wa), full(wb), full(g), full(wr),
                  full(tri), full(counts_in)],
        out_specs=(tok(D_MODEL), pl.BlockSpec((tile,) + SLAB, lambda i: (i,) + SLAB_ZEROS),
                   pl.BlockSpec((SUBLANES, tile), lambda i: (0, i)),
                   pl.BlockSpec((SUBLANES, LANES), lambda i: (0, 0))),
        scratch_shapes=[pltpu.VMEM((SUBLANES, LANES), F32)],
        compiler_params=_params("arbitrary"),
        name="outproj_router",
    )(x2d, yf2d, o2d, wa, wb, g, wr, tri, counts_in)


def _positions_kernel(starts_ref, meta_ref, pos_ref):
    cls = meta_ref[META_CLASS:META_CLASS + 1, :]
    pos = meta_ref[META_RANK:META_RANK + 1, :]
    for c in range(N_CLASSES):
        pos = pos + jnp.where(cls == c, starts_ref[c], 0)
    pos_ref[...] = jnp.broadcast_to(pos, pos_ref.shape)


def _positions(starts, meta, tile=4096):
    T = meta.shape[1]
    tile = min(tile, T)
    block = pl.BlockSpec((SUBLANES, tile), lambda i, s: (0, i))
    return pl.pallas_call(
        _positions_kernel,
        out_shape=jax.ShapeDtypeStruct(meta.shape, I32),
        grid_spec=pltpu.PrefetchScalarGridSpec(num_scalar_prefetch=1, grid=(T // tile,),
                                               in_specs=[block], out_specs=block),
        compiler_params=_params("parallel"),
        name="moe_positions",
    )(starts, meta)


def _sorted_row(pos_ref, t):
    return pos_ref[0, t]


ROW_COPY_UNROLL = 8


def _start_row_copies(n, make):
    assert n % ROW_COPY_UNROLL == 0

    def body(j, carry):
        for u in range(ROW_COPY_UNROLL):
            make(j * ROW_COPY_UNROLL + u).start()
        return carry
    lax.fori_loop(0, n // ROW_COPY_UNROLL, body, 0)


def _dispatch_kernel(fill_ref, *refs, bounds):
    n_trunks = len(bounds) - 1
    meta_refs, slab_refs = refs[0:2 * n_trunks:2], refs[1:2 * n_trunks:2]
    xs_hbm, zero_ref, sem = refs[2 * n_trunks:]
    i = pl.program_id(0)
    ftile = zero_ref.shape[0]

    @pl.when(i == 0)
    def _():
        zero_ref[...] = jnp.zeros_like(zero_ref)
        fill = lambda j: pltpu.make_async_copy(
            zero_ref, xs_hbm.at[pl.ds(pl.multiple_of(fill_ref[j] * ftile, ftile), ftile)], sem.at[1])

        def start(j, carry):
            @pl.when(fill_ref[j] >= 0)
            def _():
                fill(j).start()
            return carry

        def wait(j, carry):
            @pl.when(fill_ref[j] >= 0)
            def _():
                fill(j).wait()
            return carry

        lax.fori_loop(0, fill_ref.shape[0], start, 0)
        lax.fori_loop(0, fill_ref.shape[0], wait, 0)

    for k in range(n_trunks):
        @pl.when((i >= bounds[k]) & (i < bounds[k + 1]))
        def _(meta_ref=meta_refs[k], slab_ref=slab_refs[k]):
            tile = slab_ref.shape[0]
            _start_row_copies(tile, lambda t: pltpu.make_async_copy(
                slab_ref.at[t], xs_hbm.at[_sorted_row(meta_ref, t)], sem.at[0]))
            pltpu.make_async_copy(slab_ref, xs_hbm.at[pl.ds(0, tile)], sem.at[0]).wait()


def _dispatch(fill_tiles, metas, slabs, n_rows, tile):
    steps = [slab.shape[0] // tile for slab in slabs]
    bounds = tuple(int(b) for b in np.cumsum([0] + steps))
    in_specs, args = [], []
    for k, (meta, slab) in enumerate(zip(metas, slabs)):
        local = lambda i, k=k: jnp.clip(i - bounds[k], 0, steps[k] - 1)
        in_specs.append(pl.BlockSpec((SUBLANES, tile), lambda i, f, local=local: (0, local(i)),
                                     memory_space=pltpu.SMEM))
        in_specs.append(pl.BlockSpec((tile,) + SLAB, lambda i, f, local=local: (local(i),) + SLAB_ZEROS))
        args += [meta, slab]
    return pl.pallas_call(
        functools.partial(_dispatch_kernel, bounds=bounds),
        out_shape=jax.ShapeDtypeStruct((n_rows,) + SLAB, SLAB_DTYPE),
        grid_spec=pltpu.PrefetchScalarGridSpec(
            num_scalar_prefetch=1, grid=(bounds[-1],),
            in_specs=in_specs,
            out_specs=pl.BlockSpec(memory_space=pl.ANY),
            scratch_shapes=[pltpu.VMEM((EXPERT_TILE,) + SLAB, SLAB_DTYPE), pltpu.SemaphoreType.DMA((2,))]),
        compiler_params=_params("arbitrary"),
        name="moe_dispatch",
    )(fill_tiles, *args)


def _expert_kernel(elo_ref, ehi_ref, rows_ref, xs_ref, wg0_ref, wu0_ref, wd0_ref, wg1_ref, wu1_ref, wd1_ref,
                   ys_ref):
    i = pl.program_id(0)
    tile = xs_ref.shape[0]
    valid = rows_ref[i]

    @pl.when(valid > 0)
    def _():
        slab = xs_ref[...].reshape(tile, SLAB_WIDTH)
        x = slab[:, :D_MODEL]
        chunk = lambda j: slab[:, D_MODEL + j * LANES:D_MODEL + (j + 1) * LANES].astype(F32)
        y = None
        for k, (wg_ref, wu_ref, wd_ref) in enumerate(((wg0_ref, wu0_ref, wd0_ref), (wg1_ref, wu1_ref, wd1_ref))):
            a = _dot(x, wg_ref[0])
            hid = (a * jax.nn.sigmoid(a) * _dot(x, wu_ref[0])).astype(BF16)
            weight = jnp.tile(chunk(2 * k) + chunk(2 * k + 1), (1, D_MODEL // LANES))
            part = weight * _dot(hid, wd_ref[0])
            y = part if y is None else y + part
        ys_ref[...] = y.reshape(ys_ref.shape)

    @pl.when(valid == 0)
    def _():
        ys_ref[...] = jnp.zeros_like(ys_ref)


def _experts(tile_elo, tile_ehi, tile_rows, xs, wg, wu, wd, tile):
    n_tiles = xs.shape[0] // tile
    lo = lambda a: pl.BlockSpec((1,) + a.shape[1:], lambda i, elo, ehi, rows: (elo[i], 0, 0))
    hi = lambda a: pl.BlockSpec((1,) + a.shape[1:], lambda i, elo, ehi, rows: (ehi[i], 0, 0))
    return pl.pallas_call(
        _expert_kernel,
        out_shape=jax.ShapeDtypeStruct((xs.shape[0],) + ROW, F32),
        grid_spec=pltpu.PrefetchScalarGridSpec(
            num_scalar_prefetch=3, grid=(n_tiles,),
            in_specs=[pl.BlockSpec((tile,) + SLAB, lambda i, elo, ehi, rows: (i,) + SLAB_ZEROS),
                      lo(wg), lo(wu), lo(wd), hi(wg), hi(wu), hi(wd)],
            out_specs=pl.BlockSpec((tile,) + ROW, lambda i, elo, ehi, rows: (i, 0, 0))),
        compiler_params=_params("arbitrary"),
        name="moe_experts",
    )(tile_elo, tile_ehi, tile_rows, xs, wg, wu, wd, wg, wu, wd)


def _tail_kernel(meta_ref, meta_next_ref, ys_hbm, x1_ref, p_ref, gp_ref, wpg_ref, wpp_ref, gf_ref,
                 out_ref, buf_ref, buf2_ref, sem, *, n):
    i = pl.program_id(0)
    tile = x1_ref.shape[0]
    bufs = (buf_ref, buf2_ref)

    def row_copy(m_ref, slot, t):
        return pltpu.make_async_copy(ys_hbm.at[_sorted_row(m_ref, t)], bufs[slot].at[t], sem.at[slot])

    def wait_tile(slot):
        pltpu.make_async_copy(ys_hbm.at[pl.ds(0, tile)], bufs[slot], sem.at[slot]).wait()

    @pl.when(i == 0)
    def _():
        _start_row_copies(tile, lambda t: row_copy(meta_ref, 0, t))

    def step(slot):
        wait_tile(slot)
        for t in range(tile):
            row_copy(meta_next_ref, 1 - slot, t).start()
        ple = _dot(p_ref[...].astype(BF16), wpp_ref[...])
        x2 = x1_ref[...] + bufs[slot][...].reshape(tile, D_MODEL)
        hp = _rms(x2, gp_ref[...]).astype(BF16)
        gate = jax.nn.sigmoid(_dot(hp, wpg_ref[...]))
        out_ref[...] = _rms(x2 + ple * gate, gf_ref[...])

        @pl.when(i == n - 1)
        def _():
            wait_tile(1 - slot)

    for slot in (0, 1):
        pl.when(i % 2 == slot)(functools.partial(step, slot))


def _tail(meta, ys, x1, p2d, gp, wpg, wpp, gf, tile):
    T = x1.shape[0]
    n = T // tile
    tok = lambda w: pl.BlockSpec((tile, w), lambda i: (i, 0))
    full = lambda a: pl.BlockSpec(a.shape, lambda i: (0,) * a.ndim)
    return pl.pallas_call(
        functools.partial(_tail_kernel, n=n),
        out_shape=jax.ShapeDtypeStruct((T, D_MODEL), F32),
        grid_spec=pltpu.PrefetchScalarGridSpec(
            num_scalar_prefetch=0, grid=(n,),
            in_specs=[pl.BlockSpec((SUBLANES, tile), lambda i: (0, i), memory_space=pltpu.SMEM),
                      pl.BlockSpec((SUBLANES, tile), lambda i: (0, jnp.minimum(i + 1, n - 1)),
                                   memory_space=pltpu.SMEM),
                      pl.BlockSpec(memory_space=pl.ANY), tok(D_MODEL), tok(PLE_DIM),
                      full(gp), full(wpg), full(wpp), full(gf)],
            out_specs=tok(D_MODEL),
            scratch_shapes=[pltpu.VMEM((tile,) + ROW, F32), pltpu.VMEM((tile,) + ROW, F32),
                            pltpu.SemaphoreType.DMA((2,))]),
        compiler_params=_params("arbitrary"),
        name="combine_tail",
    )(meta, meta, ys, x1, p2d, gp, wpg, wpp, gf)


def _class_experts():
    lo, hi = [], []
    for g in range(N_GROUPS):
        for a in range(EXPERTS_PER_GROUP):
            for b in range(a + 1, EXPERTS_PER_GROUP):
                lo.append(g * EXPERTS_PER_GROUP + a)
                hi.append(g * EXPERTS_PER_GROUP + b)
    return np.asarray(lo, np.int32), np.asarray(hi, np.int32)


def _sorted_layout(counts, n_tiles, tile):
    cnt = counts[0, :N_CLASSES].astype(I32)
    padded = (cnt + tile - 1) // tile * tile
    ends = jnp.cumsum(padded)
    starts = ends - padded
    tile_start = jnp.arange(n_tiles, dtype=I32) * tile
    tile_class = jnp.minimum(jnp.sum((ends[None, :] <= tile_start[:, None]).astype(I32), axis=1), N_CLASSES - 1)
    onehot = (tile_class[:, None] == jnp.arange(N_CLASSES, dtype=I32)[None, :]).astype(I32)
    pick = lambda table: jnp.sum(onehot * table[None, :], axis=1).astype(I32)
    tile_rows = jnp.clip(pick(cnt) - (tile_start - pick(starts)), 0, tile)
    tile_rows = jnp.where(tile_start < ends[-1], tile_rows, 0).astype(I32)
    class_lo, class_hi = _class_experts()
    partial = jnp.where(padded > cnt, ends // tile - 1, -1)
    tail = ends[-1] // tile + jnp.arange(N_CLASSES, dtype=I32)
    fill_tiles = jnp.concatenate([partial, jnp.where(tail < n_tiles, tail, -1)]).astype(I32)
    starts_padded = jnp.zeros((LANES,), I32).at[:N_CLASSES].set(starts)
    return starts_padded, pick(jnp.asarray(class_lo)), pick(jnp.asarray(class_hi)), tile_rows, fill_tiles


def _mix_and_route(x, w, counts_in):
    B, S, _ = x.shape
    T = B * S
    x2d = x.reshape(T, D_MODEL)
    z = _inproj(x2d, w["g_mix"], w["w_in"], tile=512)
    z3 = z.reshape(B, S, IN_WIDTH)
    yf = _fourier(z3, w["w_fourier"])
    o = _hgrn(z3, w["lb"], w["g_o"], heads=4 if S <= 2048 else 2)
    return _outproj(x2d, yf.reshape(T, FOURIER_WIDTH), o.reshape(T, HGRN_WIDTH),
                    w["w_out_a"], w["w_out_b"], w["g_ffn"], w["w_router"], counts_in, tile=ROUTE_TILE)


def kernel(x_prompt, x_sample, p_prompt, p_sample, norm_mix, w_in, w_fourier, lb_logits, norm_o, w_out,
           norm_ffn, w_route_group, w_route_expert, w_exp_gate, w_exp_up, w_exp_down, norm_ple,
           w_ple_gate, w_ple_proj, norm_final):
    assert w_in.shape[0] == 1, "single-layer trunk"
    lb_all = jnp.cumsum(jax.nn.softmax(lb_logits.astype(F32), axis=0), axis=0)
    router = jnp.concatenate(
        [w_route_expert[0], w_route_group[0],
         jnp.zeros((D_MODEL, LANES - N_EXPERTS - N_GROUPS), F32)], axis=1)
    w = {
        "g_mix": norm_mix[0][None, :],
        "w_in": _scale_query_columns(w_in[0]).astype(BF16),
        "w_fourier": w_fourier[0],
        "lb": lb_all[0],
        "g_o": norm_o[0][None, :],
        "w_out_a": w_out[0, :FOURIER_WIDTH].astype(BF16),
        "w_out_b": w_out[0, FOURIER_WIDTH:].astype(BF16),
        "g_ffn": norm_ffn[0][None, :],
        "w_router": router.astype(BF16),
    }
    xs_in = (x_prompt, x_sample)
    ps_in = (p_prompt[0], p_sample[0])

    counts = jnp.zeros((SUBLANES, LANES), F32)
    routed = []
    for x in xs_in:
        x1, slab, meta, counts = _mix_and_route(x, w, counts)
        routed.append((x1, slab, meta))

    total = sum(x.shape[0] * x.shape[1] for x in xs_in)
    n_tiles = total // EXPERT_TILE + N_CLASSES
    starts, tile_elo, tile_ehi, tile_rows, fill_tiles = _sorted_layout(counts, n_tiles, EXPERT_TILE)

    positions = [_positions(starts, meta) for _, _, meta in routed]
    xs = _dispatch(fill_tiles, positions, [r[1] for r in routed], n_tiles * EXPERT_TILE, tile=ROUTE_TILE)
    ys = _experts(tile_elo, tile_ehi, tile_rows, xs, w_exp_gate[0].astype(BF16), w_exp_up[0].astype(BF16),
                  w_exp_down[0].astype(BF16), tile=EXPERT_TILE)

    outs = []
    for (x1, _, _), pos, x, p in zip(routed, positions, xs_in, ps_in):
        out = _tail(pos, ys, x1, p.reshape(-1, PLE_DIM), norm_ple[0][None, :],
                    w_ple_gate[0].astype(BF16), w_ple_proj[0].astype(BF16), norm_final[None, :],
                    tile=ROUTE_TILE)
        outs.append(out.reshape(x.shape))
    return tuple(outs)
```

```python
import functools
import math

import numpy as np
import jax
import jax.numpy as jnp
from jax import lax
from jax.experimental import pallas as pl
from jax.experimental.pallas import tpu as pltpu

F32 = jnp.float32
BF16 = jnp.bfloat16
I32 = jnp.int32

D_MODEL = 1024
FOURIER_WIDTH = 512
FOURIER_GROUPS = 4
GROUP_DIM = 128
HGRN_WIDTH = 512
HEAD_DIM = 128
HGRN_HEADS = 4
CHUNK = 64
IN_WIDTH = FOURIER_WIDTH + 5 * HGRN_WIDTH
N_GROUPS = 4
EXPERTS_PER_GROUP = 8
N_EXPERTS = 32
PAIRS_PER_GROUP = EXPERTS_PER_GROUP * (EXPERTS_PER_GROUP - 1) // 2
N_CLASSES = N_GROUPS * PAIRS_PER_GROUP
D_EXPERT = 512
PLE_DIM = 256
EPS = 1e-6
LANES = 128
SUBLANES = 8
VMEM_LIMIT = 56 * 1024 * 1024
ROUTE_TILE = 512
EXPERT_TILE = 256
HGRN_STEPS_PER_TRIP = 32
FOURIER_RESIDENT_ROWS = 1024
FOURIER_TABLE_ROWS = 512
ROW = (SUBLANES, LANES)
SLAB = (2 * SUBLANES, LANES)
SLAB_DTYPE = BF16
SLAB_WIDTH = 2 * D_MODEL
SLAB_ZEROS = (0,) * len(SLAB)
META_CLASS, META_RANK = 0, 1

assert N_CLASSES <= LANES


def _params(*sem):
    return pltpu.CompilerParams(dimension_semantics=sem, vmem_limit_bytes=VMEM_LIMIT)


def _dot(a, b):
    return jnp.dot(a, b, preferred_element_type=F32)


def _dot_nt(a, b):
    return lax.dot_general(a, b, (((1,), (1,)), ((), ())), preferred_element_type=F32)


def _dot_tn(a, b):
    return lax.dot_general(a, b, (((0,), (0,)), ((), ())), preferred_element_type=F32)


def _rms(x, g):
    return x * lax.rsqrt(jnp.mean(x * x, axis=-1, keepdims=True) + EPS) * g


def _inproj_kernel(x_ref, g_ref, w_ref, z_ref):
    h = _rms(x_ref[...], g_ref[...]).astype(BF16)
    for j in range(0, IN_WIDTH, D_MODEL):
        z_ref[:, j:j + D_MODEL] = _dot(h, w_ref[:, j:j + D_MODEL]).astype(z_ref.dtype)


def _inproj(x2d, g, w_bf, tile):
    T = x2d.shape[0]
    return pl.pallas_call(
        _inproj_kernel,
        out_shape=jax.ShapeDtypeStruct((T, IN_WIDTH), BF16),
        grid=(T // tile,),
        in_specs=[
            pl.BlockSpec((tile, D_MODEL), lambda i: (i, 0)),
            pl.BlockSpec((1, D_MODEL), lambda i: (0, 0)),
            pl.BlockSpec((D_MODEL, IN_WIDTH), lambda i: (0, 0)),
        ],
        out_specs=pl.BlockSpec((tile, IN_WIDTH), lambda i: (i, 0)),
        compiler_params=_params("parallel"),
        name="inproj",
    )(x2d, g, w_bf)


def _fourier_fold_kernel(cc_ref, sc_ref, w_ref, m_ref, *, scale):
    for g in range(FOURIER_GROUPS):
        w = w_ref[g]
        m_ref[g, :GROUP_DIM, :] = (jnp.dot(cc_ref[...], w, preferred_element_type=F32,
                                           precision=lax.Precision.HIGHEST) * scale).astype(m_ref.dtype)
        m_ref[g, GROUP_DIM:, :] = (jnp.dot(sc_ref[...], w, preferred_element_type=F32,
                                           precision=lax.Precision.HIGHEST) * (-scale)).astype(m_ref.dtype)


def _fourier_fold(w_fourier, seq):
    cc, sc = _dft_tables(GROUP_DIM, GROUP_DIM, 1, 0, GROUP_DIM, F32)
    scale = 1.0 / math.sqrt(seq * GROUP_DIM)
    return pl.pallas_call(
        functools.partial(_fourier_fold_kernel, scale=scale),
        out_shape=jax.ShapeDtypeStruct((FOURIER_GROUPS, 2 * GROUP_DIM, GROUP_DIM), BF16),
        name="fourier_fold",
    )(cc, sc, w_fourier)


def _dft_tables(n_rows, n_cols, row_mult, row_off, period, dtype):
    r = int(round(math.sqrt(n_rows)))
    while n_rows % r:
        r -= 1
    m = jnp.arange(n_cols, dtype=I32)[None, :]
    k_lo = jnp.arange(r, dtype=I32)[:, None] * row_mult + row_off
    k_hi = jnp.arange(n_rows // r, dtype=I32)[:, None] * (r * row_mult)
    step = 2.0 * math.pi / period
    a_lo = (((k_lo % period) * m) % period).astype(F32) * step
    a_hi = (((k_hi % period) * m) % period).astype(F32) * step
    c_lo, s_lo = jnp.cos(a_lo)[None], jnp.sin(a_lo)[None]
    c_hi, s_hi = jnp.cos(a_hi)[:, None], jnp.sin(a_hi)[:, None]
    cos = (c_hi * c_lo - s_hi * s_lo).reshape(n_rows, n_cols)
    sin = (s_hi * c_lo + c_hi * s_lo).reshape(n_rows, n_cols)
    return cos.astype(dtype), sin.astype(dtype)


def _fourier_kernel(ulo_ref, uhi_ref, ce_ref, se_ref, co_ref, so_ref, m_ref, y_ref, sum_ref, dif_ref, stage_ref):
    tk = ce_ref.shape[0]

    @pl.when(pl.program_id(1) == 0)
    def _():
        lo = ulo_ref[0].astype(F32)
        hi = uhi_ref[0].astype(F32)
        sum_ref[...] = (lo + hi).astype(BF16)
        dif_ref[...] = (lo - hi).astype(BF16)

    def mix(cos_ref, sin_ref, x, parity):
        a = _dot(cos_ref[...], x).astype(BF16)
        b = _dot(sin_ref[...], x).astype(BF16)
        for g in range(FOURIER_GROUPS):
            sl = slice(g * GROUP_DIM, (g + 1) * GROUP_DIM)
            stage_ref[g, pl.ds(parity, tk, stride=2), :] = _dot(
                jnp.concatenate([a[:, sl], b[:, sl]], axis=1), m_ref[g])

    mix(ce_ref, se_ref, sum_ref[...], 0)
    mix(co_ref, so_ref, dif_ref[...], 1)
    for g in range(FOURIER_GROUPS):
        y_ref[0, :, g * GROUP_DIM:(g + 1) * GROUP_DIM] = stage_ref[g].astype(y_ref.dtype)


def _fourier(z3, w_fourier):
    B, S, _ = z3.shape
    M = S // 2
    tk = M if M <= FOURIER_RESIDENT_ROWS else FOURIER_TABLE_ROWS
    ce, se = _dft_tables(M, M, 1, 0, M, BF16)
    co, so = _dft_tables(M, M, 2, 1, S, BF16)
    m = _fourier_fold(w_fourier, S)
    table = pl.BlockSpec((tk, M), lambda b, k: (k, 0))
    return pl.pallas_call(
        _fourier_kernel,
        out_shape=jax.ShapeDtypeStruct((B, S, FOURIER_WIDTH), BF16),
        grid=(B, M // tk),
        in_specs=[
            pl.BlockSpec((1, M, FOURIER_WIDTH), lambda b, k: (b, 0, 0)),
            pl.BlockSpec((1, M, FOURIER_WIDTH), lambda b, k: (b, 1, 0)),
            table, table, table, table,
            pl.BlockSpec((FOURIER_GROUPS, 2 * GROUP_DIM, GROUP_DIM), lambda b, k: (0, 0, 0)),
        ],
        out_specs=pl.BlockSpec((1, 2 * tk, FOURIER_WIDTH), lambda b, k: (b, k, 0)),
        scratch_shapes=[pltpu.VMEM((M, FOURIER_WIDTH), BF16), pltpu.VMEM((M, FOURIER_WIDTH), BF16),
                        pltpu.VMEM((FOURIER_GROUPS, 2 * tk, GROUP_DIM), F32)],
        compiler_params=_params("parallel", "arbitrary"),
        name="fourier",
    )(z3, z3, ce, se, co, so, m)


def _scale_query_columns(w_in):
    col = jnp.arange(IN_WIDTH)
    is_q = (col >= FOURIER_WIDTH) & (col < FOURIER_WIDTH + HGRN_WIDTH)
    return w_in * jnp.where(is_q, HEAD_DIM ** -0.5, 1.0).astype(w_in.dtype)[None, :]


def _split3(x):
    hi = x.astype(BF16)
    r1 = x - hi.astype(F32)
    mid = r1.astype(BF16)
    lo = (r1 - mid.astype(F32)).astype(BF16)
    return hi, mid, lo


def _hgrn_kernel(q_ref, v_ref, ff_ref, fb_ref, og_ref, lb_ref, go_ref, o_ref,
                 acc_ref, st_ref, qd_ref, ki_ref, kd_ref, dec_ref, *, seq, heads):
    n_chunks = seq // CHUNK
    half = n_chunks // 2
    per_trip = min(HGRN_STEPS_PER_TRIP, half)
    assert per_trip % 2 == 0 and half % per_trip == 0
    row = lax.broadcasted_iota(I32, (CHUNK, CHUNK), 0)
    col = lax.broadcasted_iota(I32, (CHUNK, CHUNK), 1)
    masks = (row >= col, row <= col)
    tris = tuple(jnp.where(m, 1.0, 0.0).astype(BF16) for m in masks)
    edges = (CHUNK - 1, 0)
    f_refs = (ff_ref, fb_ref)

    st_ref[...] = jnp.zeros_like(st_ref)

    def chunk_rows(i, d):
        c = i if d == 0 else n_chunks - 1 - i
        return pl.ds(pl.multiple_of(c * CHUNK, CHUNK), CHUNK)

    def prepare(i, d, slot):
        rows = chunk_rows(jnp.minimum(i, n_chunks - 1), d)
        q = q_ref[0, rows, :].astype(F32)
        fr = f_refs[d][0, rows, :].astype(F32)
        lb = lb_ref[d:d + 1, :]
        f = lb + (1.0 - lb) * jax.nn.sigmoid(fr)
        k = 1.0 - f
        hi, mid, lo = _split3(jnp.log(f))
        b = _dot(tris[d], hi) + _dot(tris[d], mid) + _dot(tris[d], lo)
        b_edge = b[edges[d]:edges[d] + 1, :]
        qd_ref[d, slot] = (q * jnp.exp(b)).astype(BF16)
        ki_ref[d, slot] = (k * jnp.exp(-b)).astype(BF16)
        kd_ref[d, slot] = (k * jnp.exp(b_edge - b)).astype(BF16)
        dec_ref[d, slot] = jnp.broadcast_to(jnp.exp(b_edge), dec_ref.shape[2:])

    def advance(i, d, slot):
        rows = chunk_rows(i, d)
        v = v_ref[0, rows, :]
        q_dec, k_inv, k_dec = qd_ref[d, slot], ki_ref[d, slot], kd_ref[d, slot]
        dec = dec_ref[d, slot, 0:1, :]
        outs = []
        for h in range(heads):
            sl = slice(h * HEAD_DIM, (h + 1) * HEAD_DIM)
            st = st_ref[d, h]
            scores = jnp.where(masks[d], _dot_nt(q_dec[:, sl], k_inv[:, sl]), 0.0).astype(BF16)
            outs.append(_dot(scores, v[:, sl]) + _dot_nt(q_dec[:, sl], st.astype(BF16)))
            st_ref[d, h] = st * dec[:, sl] + _dot_tn(v[:, sl], k_dec[:, sl])
        return rows, outs

    def first_touch(rows, outs):
        for h in range(heads):
            acc_ref[rows, h * HEAD_DIM:(h + 1) * HEAD_DIM] = outs[h]

    def finish(rows, outs):
        og = og_ref[0, rows, :].astype(F32)
        for h in range(heads):
            sl = slice(h * HEAD_DIM, (h + 1) * HEAD_DIM)
            o = outs[h] + acc_ref[rows, sl]
            gate = og[:, sl] * jax.nn.sigmoid(og[:, sl])
            o_ref[0, rows, sl] = (_rms(o, go_ref[...]) * gate).astype(o_ref.dtype)

    def steps(sink):
        def body(j, carry):
            for u in range(per_trip):
                i, slot = per_trip * j + u, u % 2
                for d in (0, 1):
                    prepare(i + 1, d, 1 - slot)
                for d in (0, 1):
                    sink(*advance(i, d, slot))
            return carry
        return body

    for d in (0, 1):
        prepare(0, d, 0)
    lax.fori_loop(0, half // per_trip, steps(first_touch), 0)
    lax.fori_loop(half // per_trip, n_chunks // per_trip, steps(finish), 0)


def _hgrn(z3, lb, g_o, heads):
    B, S, _ = z3.shape
    assert (S // CHUNK) % 2 == 0 and HGRN_HEADS % heads == 0
    W = heads * HEAD_DIM
    nblk = HGRN_WIDTH // W
    base = FOURIER_WIDTH // W

    def zspec(j):
        return pl.BlockSpec((1, S, W), lambda b, h, j=j: (b, 0, base + j * nblk + h))

    return pl.pallas_call(
        functools.partial(_hgrn_kernel, seq=S, heads=heads),
        out_shape=jax.ShapeDtypeStruct((B, S, HGRN_WIDTH), BF16),
        grid=(B, nblk),
        in_specs=[zspec(0), zspec(1), zspec(2), zspec(3), zspec(4),
                  pl.BlockSpec((2, W), lambda b, h: (0, h)),
                  pl.BlockSpec((1, HEAD_DIM), lambda b, h: (0, 0))],
        out_specs=pl.BlockSpec((1, S, W), lambda b, h: (b, 0, h)),
        scratch_shapes=[pltpu.VMEM((S, W), F32),
                        pltpu.VMEM((2, heads, HEAD_DIM, HEAD_DIM), F32),
                        pltpu.VMEM((2, 2, CHUNK, W), BF16),
                        pltpu.VMEM((2, 2, CHUNK, W), BF16),
                        pltpu.VMEM((2, 2, CHUNK, W), BF16),
                        pltpu.VMEM((2, 2, SUBLANES, W), F32)],
        compiler_params=_params("parallel", "arbitrary"),
        name="hgrn",
    )(z3, z3, z3, z3, z3, lb, g_o)


def _outproj_kernel(x_ref, yf_ref, o_ref, wa_ref, wb_ref, g_ref, wr_ref, tri_ref, cnt_in_ref,
                    x1_ref, slab_ref, meta_ref, counts_ref, cnt_ref):
    @pl.when(pl.program_id(0) == 0)
    def _():
        cnt_ref[...] = cnt_in_ref[...]

    tile = x_ref.shape[0]
    x1 = x_ref[...] + _dot(yf_ref[...], wa_ref[...]) + _dot(o_ref[...], wb_ref[...])
    x1_ref[...] = x1
    hb = _rms(x1, g_ref[...]).astype(BF16)
    logits = _dot(hb, wr_ref[...])
    lane = lax.broadcasted_iota(I32, logits.shape, 1)
    neg = -jnp.inf

    lane_f = lane.astype(F32)

    def argmax_lowest(x, m):
        return jnp.min(jnp.where(x == m, lane_f, float(LANES)), axis=1, keepdims=True).astype(I32)

    lg = jnp.where((lane >= N_EXPERTS) & (lane < N_EXPERTS + N_GROUPS), logits, neg)
    mg = jnp.max(lg, axis=1, keepdims=True)
    p_top = 1.0 / jnp.sum(jnp.exp(lg - mg), axis=1, keepdims=True)
    g_idx = argmax_lowest(lg, mg) - N_EXPERTS
    base = g_idx * EXPERTS_PER_GROUP
    le = jnp.where((lane >= base) & (lane < base + EXPERTS_PER_GROUP), logits, neg)
    v1 = jnp.max(le, axis=1, keepdims=True)
    i1 = argmax_lowest(le, v1)
    le2 = jnp.where(lane == i1, neg, le)
    v2 = jnp.max(le2, axis=1, keepdims=True)
    i2 = argmax_lowest(le2, v2)
    e2 = jnp.exp(v2 - v1)
    w1 = p_top / (1.0 + e2)
    w2 = w1 * e2

    a = jnp.minimum(i1, i2) - base
    b = jnp.maximum(i1, i2) - base
    pair = lax.shift_right_logical(a * (2 * EXPERTS_PER_GROUP - 1 - a), 1) + (b - a - 1)
    cls = g_idx * PAIRS_PER_GROUP + pair
    first_is_lo = i1 < i2
    w_lo = jnp.where(first_is_lo, w1, w2)
    w_hi = jnp.where(first_is_lo, w2, w1)

    onehot = jnp.where(lane == cls, 1.0, 0.0)
    before = _dot(tri_ref[...], onehot.astype(BF16))
    carry = cnt_ref[0:1, :]
    rank = jnp.sum(onehot * (carry + before), axis=1, keepdims=True)
    new_counts = jnp.broadcast_to(carry + jnp.sum(onehot, axis=0, keepdims=True), cnt_ref.shape)
    cnt_ref[...] = new_counts
    counts_ref[...] = new_counts

    def bf16_pair(w):
        w = jnp.broadcast_to(w, (tile, LANES))
        head = w.astype(BF16)
        return [head, (w - head.astype(F32)).astype(BF16)]

    slab = jnp.concatenate([hb] + bf16_pair(w_lo) + bf16_pair(w_hi)
                           + [jnp.zeros((tile, SLAB_WIDTH - D_MODEL - 4 * LANES), BF16)], axis=1)
    slab_ref[...] = slab.reshape((tile,) + SLAB)

    meta = jnp.where(lane == META_CLASS, cls, jnp.where(lane == META_RANK, rank.astype(I32), 0))
    meta_ref[...] = meta.T[0:SUBLANES, :]


def _outproj(x2d, yf2d, o2d, wa, wb, g, wr, counts_in, tile):
    T = x2d.shape[0]
    tri = jnp.tril(jnp.ones((tile, tile), F32), -1).astype(BF16)
    tok = lambda w: pl.BlockSpec((tile, w), lambda i: (i, 0))
    full = lambda a: pl.BlockSpec(a.shape, lambda i: (0,) * a.ndim)
    return pl.pallas_call(
        _outproj_kernel,
        out_shape=(jax.ShapeDtypeStruct((T, D_MODEL), F32),
                   jax.ShapeDtypeStruct((T,) + SLAB, SLAB_DTYPE),
                   jax.ShapeDtypeStruct((SUBLANES, T), I32),
                   jax.ShapeDtypeStruct((SUBLANES, LANES), F32)),
        grid=(T // tile,),
        in_specs=[tok(D_MODEL), tok(FOURIER_WIDTH), tok(HGRN_WIDTH), full(wa), full(wb), full(g), full(wr),
                  full(tri), full(counts_in)],
        out_specs=(tok(D_MODEL), pl.BlockSpec((tile,) + SLAB, lambda i: (i,) + SLAB_ZEROS),
                   pl.BlockSpec((SUBLANES, tile), lambda i: (0, i)),
                   pl.BlockSpec((SUBLANES, LANES), lambda i: (0, 0))),
        scratch_shapes=[pltpu.VMEM((SUBLANES, LANES), F32)],
        compiler_params=_params("arbitrary"),
        name="outproj_router",
    )(x2d, yf2d, o2d, wa, wb, g, wr, tri, counts_in)


def _positions_kernel(starts_ref, meta_ref, pos_ref):
    cls = meta_ref[META_CLASS:META_CLASS + 1, :]
    pos = meta_ref[META_RANK:META_RANK + 1, :]
    for c in range(N_CLASSES):
        pos = pos + jnp.where(cls == c, starts_ref[c], 0)
    pos_ref[...] = jnp.broadcast_to(pos, pos_ref.shape)


def _positions(starts, meta, tile=4096):
    T = meta.shape[1]
    tile = min(tile, T)
    block = pl.BlockSpec((SUBLANES, tile), lambda i, s: (0, i))
    return pl.pallas_call(
        _positions_kernel,
        out_shape=jax.ShapeDtypeStruct(meta.shape, I32),
        grid_spec=pltpu.PrefetchScalarGridSpec(num_scalar_prefetch=1, grid=(T // tile,),
                                               in_specs=[block], out_specs=block),
        compiler_params=_params("parallel"),
        name="moe_positions",
    )(starts, meta)


def _sorted_row(pos_ref, t):
    return pos_ref[0, t]


ROW_COPY_UNROLL = 8


def _start_row_copies(n, make):
    assert n % ROW_COPY_UNROLL == 0

    def body(j, carry):
        for u in range(ROW_COPY_UNROLL):
            make(j * ROW_COPY_UNROLL + u).start()
        return carry
    lax.fori_loop(0, n // ROW_COPY_UNROLL, body, 0)


def _dispatch_kernel(fill_ref, *refs, bounds):
    n_trunks = len(bounds) - 1
    meta_refs, slab_refs = refs[0:2 * n_trunks:2], refs[1:2 * n_trunks:2]
    xs_hbm, zero_ref, sem = refs[2 * n_trunks:]
    i = pl.program_id(0)
    ftile = zero_ref.shape[0]

    @pl.when(i == 0)
    def _():
        zero_ref[...] = jnp.zeros_like(zero_ref)
        fill = lambda j: pltpu.make_async_copy(
            zero_ref, xs_hbm.at[pl.ds(pl.multiple_of(fill_ref[j] * ftile, ftile), ftile)], sem.at[1])

        def start(j, carry):
            @pl.when(fill_ref[j] >= 0)
            def _():
                fill(j).start()
            return carry

        def wait(j, carry):
            @pl.when(fill_ref[j] >= 0)
            def _():
                fill(j).wait()
            return carry

        lax.fori_loop(0, fill_ref.shape[0], start, 0)
        lax.fori_loop(0, fill_ref.shape[0], wait, 0)

    for k in range(n_trunks):
        @pl.when((i >= bounds[k]) & (i < bounds[k + 1]))
        def _(meta_ref=meta_refs[k], slab_ref=slab_refs[k]):
            tile = slab_ref.shape[0]
            _start_row_copies(tile, lambda t: pltpu.make_async_copy(
                slab_ref.at[t], xs_hbm.at[_sorted_row(meta_ref, t)], sem.at[0]))
            pltpu.make_async_copy(slab_ref, xs_hbm.at[pl.ds(0, tile)], sem.at[0]).wait()


def _dispatch(fill_tiles, metas, slabs, n_rows, tile):
    steps = [slab.shape[0] // tile for slab in slabs]
    bounds = tuple(int(b) for b in np.cumsum([0] + steps))
    in_specs, args = [], []
    for k, (meta, slab) in enumerate(zip(metas, slabs)):
        local = lambda i, k=k: jnp.clip(i - bounds[k], 0, steps[k] - 1)
        in_specs.append(pl.BlockSpec((SUBLANES, tile), lambda i, f, local=local: (0, local(i)),
                                     memory_space=pltpu.SMEM))
        in_specs.append(pl.BlockSpec((tile,) + SLAB, lambda i, f, local=local: (local(i),) + SLAB_ZEROS))
        args += [meta, slab]
    return pl.pallas_call(
        functools.partial(_dispatch_kernel, bounds=bounds),
        out_shape=jax.ShapeDtypeStruct((n_rows,) + SLAB, SLAB_DTYPE),
        grid_spec=pltpu.PrefetchScalarGridSpec(
            num_scalar_prefetch=1, grid=(bounds[-1],),
            in_specs=in_specs,
            out_specs=pl.BlockSpec(memory_space=pl.ANY),
            scratch_shapes=[pltpu.VMEM((EXPERT_TILE,) + SLAB, SLAB_DTYPE), pltpu.SemaphoreType.DMA((2,))]),
        compiler_params=_params("arbitrary"),
        name="moe_dispatch",
    )(fill_tiles, *args)


def _expert_kernel(elo_ref, ehi_ref, rows_ref, xs_ref, wg0_ref, wu0_ref, wd0_ref, wg1_ref, wu1_ref, wd1_ref,
                   ys_ref):
    i = pl.program_id(0)
    tile = xs_ref.shape[0]
    valid = rows_ref[i]

    @pl.when(valid > 0)
    def _():
        slab = xs_ref[...].reshape(tile, SLAB_WIDTH)
        x = slab[:, :D_MODEL]
        chunk = lambda j: slab[:, D_MODEL + j * LANES:D_MODEL + (j + 1) * LANES].astype(F32)
        y = None
        for k, (wg_ref, wu_ref, wd_ref) in enumerate(((wg0_ref, wu0_ref, wd0_ref), (wg1_ref, wu1_ref, wd1_ref))):
            a = _dot(x, wg_ref[0])
            hid = (a * jax.nn.sigmoid(a) * _dot(x, wu_ref[0])).astype(BF16)
            weight = jnp.tile(chunk(2 * k) + chunk(2 * k + 1), (1, D_MODEL // LANES))
            part = weight * _dot(hid, wd_ref[0])
            y = part if y is None else y + part
        ys_ref[...] = y.reshape(ys_ref.shape)

    @pl.when(valid == 0)
    def _():
        ys_ref[...] = jnp.zeros_like(ys_ref)


def _experts(tile_elo, tile_ehi, tile_rows, xs, wg, wu, wd, tile):
    n_tiles = xs.shape[0] // tile
    lo = lambda a: pl.BlockSpec((1,) + a.shape[1:], lambda i, elo, ehi, rows: (elo[i], 0, 0))
    hi = lambda a: pl.BlockSpec((1,) + a.shape[1:], lambda i, elo, ehi, rows: (ehi[i], 0, 0))
    return pl.pallas_call(
        _expert_kernel,
        out_shape=jax.ShapeDtypeStruct((xs.shape[0],) + ROW, F32),
        grid_spec=pltpu.PrefetchScalarGridSpec(
            num_scalar_prefetch=3, grid=(n_tiles,),
            in_specs=[pl.BlockSpec((tile,) + SLAB, lambda i, elo, ehi, rows: (i,) + SLAB_ZEROS),
                      lo(wg), lo(wu), lo(wd), hi(wg), hi(wu), hi(wd)],
            out_specs=pl.BlockSpec((tile,) + ROW, lambda i, elo, ehi, rows: (i, 0, 0))),
        compiler_params=_params("arbitrary"),
        name="moe_experts",
    )(tile_elo, tile_ehi, tile_rows, xs, wg, wu, wd, wg, wu, wd)


def _tail_kernel(meta_ref, meta_next_ref, ys_hbm, x1_ref, p_ref, gp_ref, wpg_ref, wpp_ref, gf_ref,
                 out_ref, buf_ref, buf2_ref, sem, *, n):
    i = pl.program_id(0)
    tile = x1_ref.shape[0]
    bufs = (buf_ref, buf2_ref)

    def row_copy(m_ref, slot, t):
        return pltpu.make_async_copy(ys_hbm.at[_sorted_row(m_ref, t)], bufs[slot].at[t], sem.at[slot])

    def wait_tile(slot):
        pltpu.make_async_copy(ys_hbm.at[pl.ds(0, tile)], bufs[slot], sem.at[slot]).wait()

    @pl.when(i == 0)
    def _():
        _start_row_copies(tile, lambda t: row_copy(meta_ref, 0, t))

    def step(slot):
        wait_tile(slot)
        for t in range(tile):
            row_copy(meta_next_ref, 1 - slot, t).start()
        ple = _dot(p_ref[...].astype(BF16), wpp_ref[...])
        x2 = x1_ref[...] + bufs[slot][...].reshape(tile, D_MODEL)
        hp = _rms(x2, gp_ref[...]).astype(BF16)
        gate = jax.nn.sigmoid(_dot(hp, wpg_ref[...]))
        out_ref[...] = _rms(x2 + ple * gate, gf_ref[...])

        @pl.when(i == n - 1)
        def _():
            wait_tile(1 - slot)

    for slot in (0, 1):
        pl.when(i % 2 == slot)(functools.partial(step, slot))


def _tail(meta, ys, x1, p2d, gp, wpg, wpp, gf, tile):
    T = x1.shape[0]
    n = T // tile
    tok = lambda w: pl.BlockSpec((tile, w), lambda i: (i, 0))
    full = lambda a: pl.BlockSpec(a.shape, lambda i: (0,) * a.ndim)
    return pl.pallas_call(
        functools.partial(_tail_kernel, n=n),
        out_shape=jax.ShapeDtypeStruct((T, D_MODEL), F32),
        grid_spec=pltpu.PrefetchScalarGridSpec(
            num_scalar_prefetch=0, grid=(n,),
            in_specs=[pl.BlockSpec((SUBLANES, tile), lambda i: (0, i), memory_space=pltpu.SMEM),
                      pl.BlockSpec((SUBLANES, tile), lambda i: (0, jnp.minimum(i + 1, n - 1)),
                                   memory_space=pltpu.SMEM),
                      pl.BlockSpec(memory_space=pl.ANY), tok(D_MODEL), tok(PLE_DIM),
                      full(gp), full(wpg), full(wpp), full(gf)],
            out_specs=tok(D_MODEL),
            scratch_shapes=[pltpu.VMEM((tile,) + ROW, F32), pltpu.VMEM((tile,) + ROW, F32),
                            pltpu.SemaphoreType.DMA((2,))]),
        compiler_params=_params("arbitrary"),
        name="combine_tail",
    )(meta, meta, ys, x1, p2d, gp, wpg, wpp, gf)


def _class_experts():
    lo, hi = [], []
    for g in range(N_GROUPS):
        for a in range(EXPERTS_PER_GROUP):
            for b in range(a + 1, EXPERTS_PER_GROUP):
                lo.append(g * EXPERTS_PER_GROUP + a)
                hi.append(g * EXPERTS_PER_GROUP + b)
    return np.asarray(lo, np.int32), np.asarray(hi, np.int32)


def _sorted_layout(counts, n_tiles, tile):
    cnt = counts[0, :N_CLASSES].astype(I32)
    padded = (cnt + tile - 1) // tile * tile
    ends = jnp.cumsum(padded)
    starts = ends - padded
    tile_start = jnp.arange(n_tiles, dtype=I32) * tile
    tile_class = jnp.minimum(jnp.sum((ends[None, :] <= tile_start[:, None]).astype(I32), axis=1), N_CLASSES - 1)
    onehot = (tile_class[:, None] == jnp.arange(N_CLASSES, dtype=I32)[None, :]).astype(I32)
    pick = lambda table: jnp.sum(onehot * table[None, :], axis=1).astype(I32)
    tile_rows = jnp.clip(pick(cnt) - (tile_start - pick(starts)), 0, tile)
    tile_rows = jnp.where(tile_start < ends[-1], tile_rows, 0).astype(I32)
    class_lo, class_hi = _class_experts()
    partial = jnp.where(padded > cnt, ends // tile - 1, -1)
    tail = ends[-1] // tile + jnp.arange(N_CLASSES, dtype=I32)
    fill_tiles = jnp.concatenate([partial, jnp.where(tail < n_tiles, tail, -1)]).astype(I32)
    starts_padded = jnp.zeros((LANES,), I32).at[:N_CLASSES].set(starts)
    return starts_padded, pick(jnp.asarray(class_lo)), pick(jnp.asarray(class_hi)), tile_rows, fill_tiles


def _mix_and_route(x, w, counts_in):
    B, S, _ = x.shape
    T = B * S
    x2d = x.reshape(T, D_MODEL)
    z = _inproj(x2d, w["g_mix"], w["w_in"], tile=512)
    z3 = z.reshape(B, S, IN_WIDTH)
    yf = _fourier(z3, w["w_fourier"])
    o = _hgrn(z3, w["lb"], w["g_o"], heads=4 if S <= 2048 else 2)
    return _outproj(x2d, yf.reshape(T, FOURIER_WIDTH), o.reshape(T, HGRN_WIDTH),
                    w["w_out_a"], w["w_out_b"], w["g_ffn"], w["w_router"], counts_in, tile=ROUTE_TILE)


def kernel(x_prompt, x_sample, p_prompt, p_sample, norm_mix, w_in, w_fourier, lb_logits, norm_o, w_out,
           norm_ffn, w_route_group, w_route_expert, w_exp_gate, w_exp_up, w_exp_down, norm_ple,
           w_ple_gate, w_ple_proj, norm_final):
    assert w_in.shape[0] == 1, "single-layer trunk"
    lb_all = jnp.cumsum(jax.nn.softmax(lb_logits.astype(F32), axis=0), axis=0)
    router = jnp.concatenate(
        [w_route_expert[0], w_route_group[0],
         jnp.zeros((D_MODEL, LANES - N_EXPERTS - N_GROUPS), F32)], axis=1)
    w = {
        "g_mix": norm_mix[0][None, :],
        "w_in": _scale_query_columns(w_in[0]).astype(BF16),
        "w_fourier": w_fourier[0],
        "lb": lb_all[0],
        "g_o": norm_o[0][None, :],
        "w_out_a": w_out[0, :FOURIER_WIDTH].astype(BF16),
        "w_out_b": w_out[0, FOURIER_WIDTH:].astype(BF16),
        "g_ffn": norm_ffn[0][None, :],
        "w_router": router.astype(BF16),
    }
    xs_in = (x_prompt, x_sample)
    ps_in = (p_prompt[0], p_sample[0])

    counts = jnp.zeros((SUBLANES, LANES), F32)
    routed = []
    for x in xs_in:
        x1, slab, meta, counts = _mix_and_route(x, w, counts)
        routed.append((x1, slab, meta))

    total = sum(x.shape[0] * x.shape[1] for x in xs_in)
    n_tiles = total // EXPERT_TILE + N_CLASSES
    starts, tile_elo, tile_ehi, tile_rows, fill_tiles = _sorted_layout(counts, n_tiles, EXPERT_TILE)

    positions = [_positions(starts, meta) for _, _, meta in routed]
    xs = _dispatch(fill_tiles, positions, [r[1] for r in routed], n_tiles * EXPERT_TILE, tile=ROUTE_TILE)
    ys = _experts(tile_elo, tile_ehi, tile_rows, xs, w_exp_gate[0].astype(BF16), w_exp_up[0].astype(BF16),
                  w_exp_down[0].astype(BF16), tile=EXPERT_TILE)

    outs = []
    for (x1, _, _), pos, x, p in zip(routed, positions, xs_in, ps_in):
        out = _tail(pos, ys, x1, p.reshape(-1, PLE_DIM), norm_ple[0][None, :],
                    w_ple_gate[0].astype(BF16), w_ple_proj[0].astype(BF16), norm_final[None, :],
                    tile=ROUTE_TILE)
        outs.append(out.reshape(x.shape))
    return tuple(outs)
```

```python
import functools
import math

import numpy as np
import jax
import jax.numpy as jnp
from jax import lax
from jax.experimental import pallas as pl
from jax.experimental.pallas import tpu as pltpu

F32 = jnp.float32
BF16 = jnp.bfloat16
I32 = jnp.int32

D_MODEL = 1024
FOURIER_WIDTH = 512
FOURIER_GROUPS = 4
GROUP_DIM = 128
HGRN_WIDTH = 512
HEAD_DIM = 128
HGRN_HEADS = 4
CHUNK = 64
IN_WIDTH = FOURIER_WIDTH + 5 * HGRN_WIDTH
N_GROUPS = 4
EXPERTS_PER_GROUP = 8
N_EXPERTS = 32
PAIRS_PER_GROUP = EXPERTS_PER_GROUP * (EXPERTS_PER_GROUP - 1) // 2
N_CLASSES = N_GROUPS * PAIRS_PER_GROUP
D_EXPERT = 512
PLE_DIM = 256
EPS = 1e-6
LANES = 128
SUBLANES = 8
VMEM_LIMIT = 56 * 1024 * 1024
ROUTE_TILE = 512
EXPERT_TILE = 256
HGRN_STEPS_PER_TRIP = 32
FOURIER_RESIDENT_ROWS = 1024
FOURIER_TABLE_ROWS = 512
ROW = (SUBLANES, LANES)
SLAB = (2 * SUBLANES, LANES)
SLAB_DTYPE = BF16
SLAB_WIDTH = 2 * D_MODEL
SLAB_ZEROS = (0,) * len(SLAB)
META_CLASS, META_RANK = 0, 1

assert N_CLASSES <= LANES


def _params(*sem):
    return pltpu.CompilerParams(dimension_semantics=sem, vmem_limit_bytes=VMEM_LIMIT)


def _dot(a, b):
    return jnp.dot(a, b, preferred_element_type=F32)


def _dot_nt(a, b):
    return lax.dot_general(a, b, (((1,), (1,)), ((), ())), preferred_element_type=F32)


def _dot_tn(a, b):
    return lax.dot_general(a, b, (((0,), (0,)), ((), ())), preferred_element_type=F32)


def _rms(x, g):
    return x * lax.rsqrt(jnp.mean(x * x, axis=-1, keepdims=True) + EPS) * g


def _inproj_kernel(x_ref, g_ref, w_ref, z_ref):
    h = _rms(x_ref[...], g_ref[...]).astype(BF16)
    for j in range(0, IN_WIDTH, D_MODEL):
        z_ref[:, j:j + D_MODEL] = _dot(h, w_ref[:, j:j + D_MODEL]).astype(z_ref.dtype)


def _inproj(x2d, g, w_bf, tile):
    T = x2d.shape[0]
    return pl.pallas_call(
        _inproj_kernel,
        out_shape=jax.ShapeDtypeStruct((T, IN_WIDTH), BF16),
        grid=(T // tile,),
        in_specs=[
            pl.BlockSpec((tile, D_MODEL), lambda i: (i, 0)),
            pl.BlockSpec((1, D_MODEL), lambda i: (0, 0)),
            pl.BlockSpec((D_MODEL, IN_WIDTH), lambda i: (0, 0)),
        ],
        out_specs=pl.BlockSpec((tile, IN_WIDTH), lambda i: (i, 0)),
        compiler_params=_params("parallel"),
        name="inproj",
    )(x2d, g, w_bf)


def _fourier_fold_kernel(cc_ref, sc_ref, w_ref, m_ref, *, scale):
    for g in range(FOURIER_GROUPS):
        w = w_ref[g]
        m_ref[g, :GROUP_DIM, :] = (jnp.dot(cc_ref[...], w, preferred_element_type=F32,
                                           precision=lax.Precision.HIGHEST) * scale).astype(m_ref.dtype)
        m_ref[g, GROUP_DIM:, :] = (jnp.dot(sc_ref[...], w, preferred_element_type=F32,
                                           precision=lax.Precision.HIGHEST) * (-scale)).astype(m_ref.dtype)


def _fourier_fold(w_fourier, seq):
    cc, sc = _dft_tables(GROUP_DIM, GROUP_DIM, 1, 0, GROUP_DIM, F32)
    scale = 1.0 / math.sqrt(seq * GROUP_DIM)
    return pl.pallas_call(
        functools.partial(_fourier_fold_kernel, scale=scale),
        out_shape=jax.ShapeDtypeStruct((FOURIER_GROUPS, 2 * GROUP_DIM, GROUP_DIM), BF16),
        name="fourier_fold",
    )(cc, sc, w_fourier)


def _dft_tables(n_rows, n_cols, row_mult, row_off, period, dtype):
    r = int(round(math.sqrt(n_rows)))
    while n_rows % r:
        r -= 1
    m = jnp.arange(n_cols, dtype=I32)[None, :]
    k_lo = jnp.arange(r, dtype=I32)[:, None] * row_mult + row_off
    k_hi = jnp.arange(n_rows // r, dtype=I32)[:, None] * (r * row_mult)
    step = 2.0 * math.pi / period
    a_lo = (((k_lo % period) * m) % period).astype(F32) * step
    a_hi = (((k_hi % period) * m) % period).astype(F32) * step
    c_lo, s_lo = jnp.cos(a_lo)[None], jnp.sin(a_lo)[None]
    c_hi, s_hi = jnp.cos(a_hi)[:, None], jnp.sin(a_hi)[:, None]
    cos = (c_hi * c_lo - s_hi * s_lo).reshape(n_rows, n_cols)
    sin = (s_hi * c_lo + c_hi * s_lo).reshape(n_rows, n_cols)
    return cos.astype(dtype), sin.astype(dtype)


def _fourier_kernel(ulo_ref, uhi_ref, ce_ref, se_ref, co_ref, so_ref, m_ref, y_ref, sum_ref, dif_ref, stage_ref):
    tk = ce_ref.shape[0]

    @pl.when(pl.program_id(1) == 0)
    def _():
        lo = ulo_ref[0].astype(F32)
        hi = uhi_ref[0].astype(F32)
        sum_ref[...] = (lo + hi).astype(BF16)
        dif_ref[...] = (lo - hi).astype(BF16)

    def mix(cos_ref, sin_ref, x, parity):
        a = _dot(cos_ref[...], x).astype(BF16)
        b = _dot(sin_ref[...], x).astype(BF16)
        for g in range(FOURIER_GROUPS):
            sl = slice(g * GROUP_DIM, (g + 1) * GROUP_DIM)
            stage_ref[g, pl.ds(parity, tk, stride=2), :] = _dot(
                jnp.concatenate([a[:, sl], b[:, sl]], axis=1), m_ref[g])

    mix(ce_ref, se_ref, sum_ref[...], 0)
    mix(co_ref, so_ref, dif_ref[...], 1)
    for g in range(FOURIER_GROUPS):
        y_ref[0, :, g * GROUP_DIM:(g + 1) * GROUP_DIM] = stage_ref[g].astype(y_ref.dtype)


def _fourier(z3, w_fourier):
    B, S, _ = z3.shape
    M = S // 2
    tk = M if M <= FOURIER_RESIDENT_ROWS else FOURIER_TABLE_ROWS
    ce, se = _dft_tables(M, M, 1, 0, M, BF16)
    co, so = _dft_tables(M, M, 2, 1, S, BF16)
    m = _fourier_fold(w_fourier, S)
    table = pl.BlockSpec((tk, M), lambda b, k: (k, 0))
    return pl.pallas_call(
        _fourier_kernel,
        out_shape=jax.ShapeDtypeStruct((B, S, FOURIER_WIDTH), BF16),
        grid=(B, M // tk),
        in_specs=[
            pl.BlockSpec((1, M, FOURIER_WIDTH), lambda b, k: (b, 0, 0)),
            pl.BlockSpec((1, M, FOURIER_WIDTH), lambda b, k: (b, 1, 0)),
            table, table, table, table,
            pl.BlockSpec((FOURIER_GROUPS, 2 * GROUP_DIM, GROUP_DIM), lambda b, k: (0, 0, 0)),
        ],
        out_specs=pl.BlockSpec((1, 2 * tk, FOURIER_WIDTH), lambda b, k: (b, k, 0)),
        scratch_shapes=[pltpu.VMEM((M, FOURIER_WIDTH), BF16), pltpu.VMEM((M, FOURIER_WIDTH), BF16),
                        pltpu.VMEM((FOURIER_GROUPS, 2 * tk, GROUP_DIM), F32)],
        compiler_params=_params("parallel", "arbitrary"),
        name="fourier",
    )(z3, z3, ce, se, co, so, m)


def _scale_query_columns(w_in):
    col = jnp.arange(IN_WIDTH)
    is_q = (col >= FOURIER_WIDTH) & (col < FOURIER_WIDTH + HGRN_WIDTH)
    return w_in * jnp.where(is_q, HEAD_DIM ** -0.5, 1.0).astype(w_in.dtype)[None, :]


def _split3(x):
    hi = x.astype(BF16)
    r1 = x - hi.astype(F32)
    mid = r1.astype(BF16)
    lo = (r1 - mid.astype(F32)).astype(BF16)
    return hi, mid, lo


def _hgrn_kernel(q_ref, v_ref, ff_ref, fb_ref, og_ref, lb_ref, go_ref, o_ref,
                 acc_ref, st_ref, qd_ref, ki_ref, kd_ref, dec_ref, *, seq, heads):
    n_chunks = seq // CHUNK
    half = n_chunks // 2
    per_trip = min(HGRN_STEPS_PER_TRIP, half)
    assert per_trip % 2 == 0 and half % per_trip == 0
    row = lax.broadcasted_iota(I32, (CHUNK, CHUNK), 0)
    col = lax.broadcasted_iota(I32, (CHUNK, CHUNK), 1)
    masks = (row >= col, row <= col)
    tris = tuple(jnp.where(m, 1.0, 0.0).astype(BF16) for m in masks)
    edges = (CHUNK - 1, 0)
    f_refs = (ff_ref, fb_ref)

    st_ref[...] = jnp.zeros_like(st_ref)

    def chunk_rows(i, d):
        c = i if d == 0 else n_chunks - 1 - i
        return pl.ds(pl.multiple_of(c * CHUNK, CHUNK), CHUNK)

    def prepare(i, d, slot):
        rows = chunk_rows(jnp.minimum(i, n_chunks - 1), d)
        q = q_ref[0, rows, :].astype(F32)
        fr = f_refs[d][0, rows, :].astype(F32)
        lb = lb_ref[d:d + 1, :]
        f = lb + (1.0 - lb) * jax.nn.sigmoid(fr)
        k = 1.0 - f
        hi, mid, lo = _split3(jnp.log(f))
        b = _dot(tris[d], hi) + _dot(tris[d], mid) + _dot(tris[d], lo)
        b_edge = b[edges[d]:edges[d] + 1, :]
        qd_ref[d, slot] = (q * jnp.exp(b)).astype(BF16)
        ki_ref[d, slot] = (k * jnp.exp(-b)).astype(BF16)
        kd_ref[d, slot] = (k * jnp.exp(b_edge - b)).astype(BF16)
        dec_ref[d, slot] = jnp.broadcast_to(jnp.exp(b_edge), dec_ref.shape[2:])

    def advance(i, d, slot):
        rows = chunk_rows(i, d)
        v = v_ref[0, rows, :]
        q_dec, k_inv, k_dec = qd_ref[d, slot], ki_ref[d, slot], kd_ref[d, slot]
        dec = dec_ref[d, slot, 0:1, :]
        outs = []
        for h in range(heads):
            sl = slice(h * HEAD_DIM, (h + 1) * HEAD_DIM)
            st = st_ref[d, h]
            scores = jnp.where(masks[d], _dot_nt(q_dec[:, sl], k_inv[:, sl]), 0.0).astype(BF16)
            outs.append(_dot(scores, v[:, sl]) + _dot_nt(q_dec[:, sl], st.astype(BF16)))
            st_ref[d, h] = st * dec[:, sl] + _dot_tn(v[:, sl], k_dec[:, sl])
        return rows, outs

    def first_touch(rows, outs):
        for h in range(heads):
            acc_ref[rows, h * HEAD_DIM:(h + 1) * HEAD_DIM] = outs[h]

    def finish(rows, outs):
        og = og_ref[0, rows, :].astype(F32)
        for h in range(heads):
            sl = slice(h * HEAD_DIM, (h + 1) * HEAD_DIM)
            o = outs[h] + acc_ref[rows, sl]
            gate = og[:, sl] * jax.nn.sigmoid(og[:, sl])
            o_ref[0, rows, sl] = (_rms(o, go_ref[...]) * gate).astype(o_ref.dtype)

    def steps(sink):
        def body(j, carry):
            for u in range(per_trip):
                i, slot = per_trip * j + u, u % 2
                for d in (0, 1):
                    prepare(i + 1, d, 1 - slot)
                for d in (0, 1):
                    sink(*advance(i, d, slot))
            return carry
        return body

    for d in (0, 1):
        prepare(0, d, 0)
    lax.fori_loop(0, half // per_trip, steps(first_touch), 0)
    lax.fori_loop(half // per_trip, n_chunks // per_trip, steps(finish), 0)


def _hgrn(z3, lb, g_o, heads):
    B, S, _ = z3.shape
    assert (S // CHUNK) % 2 == 0 and HGRN_HEADS % heads == 0
    W = heads * HEAD_DIM
    nblk = HGRN_WIDTH // W
    base = FOURIER_WIDTH // W

    def zspec(j):
        return pl.BlockSpec((1, S, W), lambda b, h, j=j: (b, 0, base + j * nblk + h))

    return pl.pallas_call(
        functools.partial(_hgrn_kernel, seq=S, heads=heads),
        out_shape=jax.ShapeDtypeStruct((B, S, HGRN_WIDTH), BF16),
        grid=(B, nblk),
        in_specs=[zspec(0), zspec(1), zspec(2), zspec(3), zspec(4),
                  pl.BlockSpec((2, W), lambda b, h: (0, h)),
                  pl.BlockSpec((1, HEAD_DIM), lambda b, h: (0, 0))],
        out_specs=pl.BlockSpec((1, S, W), lambda b, h: (b, 0, h)),
        scratch_shapes=[pltpu.VMEM((S, W), F32),
                        pltpu.VMEM((2, heads, HEAD_DIM, HEAD_DIM), F32),
                        pltpu.VMEM((2, 2, CHUNK, W), BF16),
                        pltpu.VMEM((2, 2, CHUNK, W), BF16),
                        pltpu.VMEM((2, 2, CHUNK, W), BF16),
                        pltpu.VMEM((2, 2, SUBLANES, W), F32)],
        compiler_params=_params("parallel", "arbitrary"),
        name="hgrn",
    )(z3, z3, z3, z3, z3, lb, g_o)


def _outproj_kernel(x_ref, yf_ref, o_ref, wa_ref, wb_ref, g_ref, wr_ref, tri_ref, cnt_in_ref,
                    x1_ref, slab_ref, meta_ref, counts_ref, cnt_ref):
    @pl.when(pl.program_id(0) == 0)
    def _():
        cnt_ref[...] = cnt_in_ref[...]

    tile = x_ref.shape[0]
    x1 = x_ref[...] + _dot(yf_ref[...], wa_ref[...]) + _dot(o_ref[...], wb_ref[...])
    x1_ref[...] = x1
    hb = _rms(x1, g_ref[...]).astype(BF16)
    logits = _dot(hb, wr_ref[...])
    lane = lax.broadcasted_iota(I32, logits.shape, 1)
    neg = -jnp.inf

    lane_f = lane.astype(F32)

    def argmax_lowest(x, m):
        return jnp.min(jnp.where(x == m, lane_f, float(LANES)), axis=1, keepdims=True).astype(I32)

    lg = jnp.where((lane >= N_EXPERTS) & (lane < N_EXPERTS + N_GROUPS), logits, neg)
    mg = jnp.max(lg, axis=1, keepdims=True)
    p_top = 1.0 / jnp.sum(jnp.exp(lg - mg), axis=1, keepdims=True)
    g_idx = argmax_lowest(lg, mg) - N_EXPERTS
    base = g_idx * EXPERTS_PER_GROUP
    le = jnp.where((lane >= base) & (lane < base + EXPERTS_PER_GROUP), logits, neg)
    v1 = jnp.max(le, axis=1, keepdims=True)
    i1 = argmax_lowest(le, v1)
    le2 = jnp.where(lane == i1, neg, le)
    v2 = jnp.max(le2, axis=1, keepdims=True)
    i2 = argmax_lowest(le2, v2)
    e2 = jnp.exp(v2 - v1)
    w1 = p_top / (1.0 + e2)
    w2 = w1 * e2

    a = jnp.minimum(i1, i2) - base
    b = jnp.maximum(i1, i2) - base
    pair = lax.shift_right_logical(a * (2 * EXPERTS_PER_GROUP - 1 - a), 1) + (b - a - 1)
    cls = g_idx * PAIRS_PER_GROUP + pair
    first_is_lo = i1 < i2
    w_lo = jnp.where(first_is_lo, w1, w2)
    w_hi = jnp.where(first_is_lo, w2, w1)

    onehot = jnp.where(lane == cls, 1.0, 0.0)
    before = _dot(tri_ref[...], onehot.astype(BF16))
    carry = cnt_ref[0:1, :]
    rank = jnp.sum(onehot * (carry + before), axis=1, keepdims=True)
    new_counts = jnp.broadcast_to(carry + jnp.sum(onehot, axis=0, keepdims=True), cnt_ref.shape)
    cnt_ref[...] = new_counts
    counts_ref[...] = new_counts

    def bf16_pair(w):
        w = jnp.broadcast_to(w, (tile, LANES))
        head = w.astype(BF16)
        return [head, (w - head.astype(F32)).astype(BF16)]

    slab = jnp.concatenate([hb] + bf16_pair(w_lo) + bf16_pair(w_hi)
                           + [jnp.zeros((tile, SLAB_WIDTH - D_MODEL - 4 * LANES), BF16)], axis=1)
    slab_ref[...] = slab.reshape((tile,) + SLAB)

    meta = jnp.where(lane == META_CLASS, cls, jnp.where(lane == META_RANK, rank.astype(I32), 0))
    meta_ref[...] = meta.T[0:SUBLANES, :]


def _outproj(x2d, yf2d, o2d, wa, wb, g, wr, counts_in, tile):
    T = x2d.shape[0]
    tri = jnp.tril(jnp.ones((tile, tile), F32), -1).astype(BF16)
    tok = lambda w: pl.BlockSpec((tile, w), lambda i: (i, 0))
    full = lambda a: pl.BlockSpec(a.shape, lambda i: (0,) * a.ndim)
    return pl.pallas_call(
        _outproj_kernel,
        out_shape=(jax.ShapeDtypeStruct((T, D_MODEL), F32),
                   jax.ShapeDtypeStruct((T,) + SLAB, SLAB_DTYPE),
                   jax.ShapeDtypeStruct((SUBLANES, T), I32),
                   jax.ShapeDtypeStruct((SUBLANES, LANES), F32)),
        grid=(T // tile,),
        in_specs=[tok(D_MODEL), tok(FOURIER_WIDTH), tok(HGRN_WIDTH), full(wa), full(wb), full(g), full(wr),
                  full(tri), full(counts_in)],
        out_specs=(tok(D_MODEL), pl.BlockSpec((tile,) + SLAB, lambda i: (i,) + SLAB_ZEROS),
                   pl.BlockSpec((SUBLANES, tile), lambda i: (0, i)),
                   pl.BlockSpec((SUBLANES, LANES), lambda i: (0, 0))),
        scratch_shapes=[pltpu.VMEM((SUBLANES, LANES), F32)],
        compiler_params=_params("arbitrary"),
        name="outproj_router",
    )(x2d, yf2d, o2d, wa, wb, g, wr, tri, counts_in)


def _positions_kernel(starts_ref, meta_ref, pos_ref):
    cls = meta_ref[META_CLASS:META_CLASS + 1, :]
    pos = meta_ref[META_RANK:META_RANK + 1, :]
    for c in range(N_CLASSES):
        pos = pos + jnp.where(cls == c, starts_ref[c], 0)
    pos_ref[...] = jnp.broadcast_to(pos, pos_ref.shape)


def _positions(starts, meta, tile=4096):
    T = meta.shape[1]
    tile = min(tile, T)
    block = pl.BlockSpec((SUBLANES, tile), lambda i, s: (0, i))
    return pl.pallas_call(
        _positions_kernel,
        out_shape=jax.ShapeDtypeStruct(meta.shape, I32),
        grid_spec=pltpu.PrefetchScalarGridSpec(num_scalar_prefetch=1, grid=(T // tile,),
                                               in_specs=[block], out_specs=block),
        compiler_params=_params("parallel"),
        name="moe_positions",
    )(starts, meta)


def _sorted_row(pos_ref, t):
    return pos_ref[0, t]


ROW_COPY_UNROLL = 8


def _start_row_copies(n, make):
    assert n % ROW_COPY_UNROLL == 0

    def body(j, carry):
        for u in range(ROW_COPY_UNROLL):
            make(j * ROW_COPY_UNROLL + u).start(priority=u % 2)
        return carry
    lax.fori_loop(0, n // ROW_COPY_UNROLL, body, 0)


def _dispatch_kernel(fill_ref, *refs, bounds):
    n_trunks = len(bounds) - 1
    meta_refs, slab_refs = refs[0:2 * n_trunks:2], refs[1:2 * n_trunks:2]
    xs_hbm, zero_ref, sem = refs[2 * n_trunks:]
    i = pl.program_id(0)
    ftile = zero_ref.shape[0]

    @pl.when(i == 0)
    def _():
        zero_ref[...] = jnp.zeros_like(zero_ref)
        fill = lambda j: pltpu.make_async_copy(
            zero_ref, xs_hbm.at[pl.ds(pl.multiple_of(fill_ref[j] * ftile, ftile), ftile)], sem.at[1])

        def start(j, carry):
            @pl.when(fill_ref[j] >= 0)
            def _():
                fill(j).start()
            return carry

        def wait(j, carry):
            @pl.when(fill_ref[j] >= 0)
            def _():
                fill(j).wait()
            return carry

        lax.fori_loop(0, fill_ref.shape[0], start, 0)
        lax.fori_loop(0, fill_ref.shape[0], wait, 0)

    for k in range(n_trunks):
        @pl.when((i >= bounds[k]) & (i < bounds[k + 1]))
        def _(meta_ref=meta_refs[k], slab_ref=slab_refs[k]):
            tile = slab_ref.shape[0]
            _start_row_copies(tile, lambda t: pltpu.make_async_copy(
                slab_ref.at[t], xs_hbm.at[_sorted_row(meta_ref, t)], sem.at[0]))
            pltpu.make_async_copy(slab_ref, xs_hbm.at[pl.ds(0, tile)], sem.at[0]).wait()


def _dispatch(fill_tiles, metas, slabs, n_rows, tile):
    steps = [slab.shape[0] // tile for slab in slabs]
    bounds = tuple(int(b) for b in np.cumsum([0] + steps))
    in_specs, args = [], []
    for k, (meta, slab) in enumerate(zip(metas, slabs)):
        local = lambda i, k=k: jnp.clip(i - bounds[k], 0, steps[k] - 1)
        in_specs.append(pl.BlockSpec((SUBLANES, tile), lambda i, f, local=local: (0, local(i)),
                                     memory_space=pltpu.SMEM))
        in_specs.append(pl.BlockSpec((tile,) + SLAB, lambda i, f, local=local: (local(i),) + SLAB_ZEROS))
        args += [meta, slab]
    return pl.pallas_call(
        functools.partial(_dispatch_kernel, bounds=bounds),
        out_shape=jax.ShapeDtypeStruct((n_rows,) + SLAB, SLAB_DTYPE),
        grid_spec=pltpu.PrefetchScalarGridSpec(
            num_scalar_prefetch=1, grid=(bounds[-1],),
            in_specs=in_specs,
            out_specs=pl.BlockSpec(memory_space=pl.ANY),
            scratch_shapes=[pltpu.VMEM((EXPERT_TILE,) + SLAB, SLAB_DTYPE), pltpu.SemaphoreType.DMA((2,))]),
        compiler_params=_params("arbitrary"),
        name="moe_dispatch",
    )(fill_tiles, *args)


def _expert_kernel(elo_ref, ehi_ref, rows_ref, xs_ref, wg0_ref, wu0_ref, wd0_ref, wg1_ref, wu1_ref, wd1_ref,
                   ys_ref):
    i = pl.program_id(0)
    tile = xs_ref.shape[0]
    valid = rows_ref[i]

    @pl.when(valid > 0)
    def _():
        slab = xs_ref[...].reshape(tile, SLAB_WIDTH)
        x = slab[:, :D_MODEL]
        chunk = lambda j: slab[:, D_MODEL + j * LANES:D_MODEL + (j + 1) * LANES].astype(F32)
        y = None
        for k, (wg_ref, wu_ref, wd_ref) in enumerate(((wg0_ref, wu0_ref, wd0_ref), (wg1_ref, wu1_ref, wd1_ref))):
            a = _dot(x, wg_ref[0])
            hid = (a * jax.nn.sigmoid(a) * _dot(x, wu_ref[0])).astype(BF16)
            weight = jnp.tile(chunk(2 * k) + chunk(2 * k + 1), (1, D_MODEL // LANES))
            part = weight * _dot(hid, wd_ref[0])
            y = part if y is None else y + part
        ys_ref[...] = y.reshape(ys_ref.shape)

    @pl.when(valid == 0)
    def _():
        ys_ref[...] = jnp.zeros_like(ys_ref)


def _experts(tile_elo, tile_ehi, tile_rows, xs, wg, wu, wd, tile):
    n_tiles = xs.shape[0] // tile
    lo = lambda a: pl.BlockSpec((1,) + a.shape[1:], lambda i, elo, ehi, rows: (elo[i], 0, 0))
    hi = lambda a: pl.BlockSpec((1,) + a.shape[1:], lambda i, elo, ehi, rows: (ehi[i], 0, 0))
    return pl.pallas_call(
        _expert_kernel,
        out_shape=jax.ShapeDtypeStruct((xs.shape[0],) + ROW, F32),
        grid_spec=pltpu.PrefetchScalarGridSpec(
            num_scalar_prefetch=3, grid=(n_tiles,),
            in_specs=[pl.BlockSpec((tile,) + SLAB, lambda i, elo, ehi, rows: (i,) + SLAB_ZEROS),
                      lo(wg), lo(wu), lo(wd), hi(wg), hi(wu), hi(wd)],
            out_specs=pl.BlockSpec((tile,) + ROW, lambda i, elo, ehi, rows: (i, 0, 0))),
        compiler_params=_params("arbitrary"),
        name="moe_experts",
    )(tile_elo, tile_ehi, tile_rows, xs, wg, wu, wd, wg, wu, wd)


def _tail_kernel(meta_ref, meta_next_ref, ys_hbm, x1_ref, p_ref, gp_ref, wpg_ref, wpp_ref, gf_ref,
                 out_ref, buf_ref, buf2_ref, sem, *, n):
    i = pl.program_id(0)
    tile = x1_ref.shape[0]
    bufs = (buf_ref, buf2_ref)

    def row_copy(m_ref, slot, t):
        return pltpu.make_async_copy(ys_hbm.at[_sorted_row(m_ref, t)], bufs[slot].at[t], sem.at[slot])

    def wait_tile(slot):
        pltpu.make_async_copy(ys_hbm.at[pl.ds(0, tile)], bufs[slot], sem.at[slot]).wait()

    @pl.when(i == 0)
    def _():
        _start_row_copies(tile, lambda t: row_copy(meta_ref, 0, t))

    def step(slot):
        wait_tile(slot)
        for t in range(tile):
            row_copy(meta_next_ref, 1 - slot, t).start()
        ple = _dot(p_ref[...].astype(BF16), wpp_ref[...])
        x2 = x1_ref[...] + bufs[slot][...].reshape(tile, D_MODEL)
        hp = _rms(x2, gp_ref[...]).astype(BF16)
        gate = jax.nn.sigmoid(_dot(hp, wpg_ref[...]))
        out_ref[...] = _rms(x2 + ple * gate, gf_ref[...])

        @pl.when(i == n - 1)
        def _():
            wait_tile(1 - slot)

    for slot in (0, 1):
        pl.when(i % 2 == slot)(functools.partial(step, slot))


def _tail(meta, ys, x1, p2d, gp, wpg, wpp, gf, tile):
    T = x1.shape[0]
    n = T // tile
    tok = lambda w: pl.BlockSpec((tile, w), lambda i: (i, 0))
    full = lambda a: pl.BlockSpec(a.shape, lambda i: (0,) * a.ndim)
    return pl.pallas_call(
        functools.partial(_tail_kernel, n=n),
        out_shape=jax.ShapeDtypeStruct((T, D_MODEL), F32),
        grid_spec=pltpu.PrefetchScalarGridSpec(
            num_scalar_prefetch=0, grid=(n,),
            in_specs=[pl.BlockSpec((SUBLANES, tile), lambda i: (0, i), memory_space=pltpu.SMEM),
                      pl.BlockSpec((SUBLANES, tile), lambda i: (0, jnp.minimum(i + 1, n - 1)),
                                   memory_space=pltpu.SMEM),
                      pl.BlockSpec(memory_space=pl.ANY), tok(D_MODEL), tok(PLE_DIM),
                      full(gp), full(wpg), full(wpp), full(gf)],
            out_specs=tok(D_MODEL),
            scratch_shapes=[pltpu.VMEM((tile,) + ROW, F32), pltpu.VMEM((tile,) + ROW, F32),
                            pltpu.SemaphoreType.DMA((2,))]),
        compiler_params=_params("arbitrary"),
        name="combine_tail",
    )(meta, meta, ys, x1, p2d, gp, wpg, wpp, gf)


def _class_experts():
    lo, hi = [], []
    for g in range(N_GROUPS):
        for a in range(EXPERTS_PER_GROUP):
            for b in range(a + 1, EXPERTS_PER_GROUP):
                lo.append(g * EXPERTS_PER_GROUP + a)
                hi.append(g * EXPERTS_PER_GROUP + b)
    return np.asarray(lo, np.int32), np.asarray(hi, np.int32)


def _sorted_layout(counts, n_tiles, tile):
    cnt = counts[0, :N_CLASSES].astype(I32)
    padded = (cnt + tile - 1) // tile * tile
    ends = jnp.cumsum(padded)
    starts = ends - padded
    tile_start = jnp.arange(n_tiles, dtype=I32) * tile
    tile_class = jnp.minimum(jnp.sum((ends[None, :] <= tile_start[:, None]).astype(I32), axis=1), N_CLASSES - 1)
    onehot = (tile_class[:, None] == jnp.arange(N_CLASSES, dtype=I32)[None, :]).astype(I32)
    pick = lambda table: jnp.sum(onehot * table[None, :], axis=1).astype(I32)
    tile_rows = jnp.clip(pick(cnt) - (tile_start - pick(starts)), 0, tile)
    tile_rows = jnp.where(tile_start < ends[-1], tile_rows, 0).astype(I32)
    class_lo, class_hi = _class_experts()
    partial = jnp.where(padded > cnt, ends // tile - 1, -1)
    tail = ends[-1] // tile + jnp.arange(N_CLASSES, dtype=I32)
    fill_tiles = jnp.concatenate([partial, jnp.where(tail < n_tiles, tail, -1)]).astype(I32)
    starts_padded = jnp.zeros((LANES,), I32).at[:N_CLASSES].set(starts)
    return starts_padded, pick(jnp.asarray(class_lo)), pick(jnp.asarray(class_hi)), tile_rows, fill_tiles


def _mix_and_route(x, w, counts_in):
    B, S, _ = x.shape
    T = B * S
    x2d = x.reshape(T, D_MODEL)
    z = _inproj(x2d, w["g_mix"], w["w_in"], tile=512)
    z3 = z.reshape(B, S, IN_WIDTH)
    yf = _fourier(z3, w["w_fourier"])
    o = _hgrn(z3, w["lb"], w["g_o"], heads=4 if S <= 2048 else 2)
    return _outproj(x2d, yf.reshape(T, FOURIER_WIDTH), o.reshape(T, HGRN_WIDTH),
                    w["w_out_a"], w["w_out_b"], w["g_ffn"], w["w_router"], counts_in, tile=ROUTE_TILE)


def kernel(x_prompt, x_sample, p_prompt, p_sample, norm_mix, w_in, w_fourier, lb_logits, norm_o, w_out,
           norm_ffn, w_route_group, w_route_expert, w_exp_gate, w_exp_up, w_exp_down, norm_ple,
           w_ple_gate, w_ple_proj, norm_final):
    assert w_in.shape[0] == 1, "single-layer trunk"
    lb_all = jnp.cumsum(jax.nn.softmax(lb_logits.astype(F32), axis=0), axis=0)
    router = jnp.concatenate(
        [w_route_expert[0], w_route_group[0],
         jnp.zeros((D_MODEL, LANES - N_EXPERTS - N_GROUPS), F32)], axis=1)
    w = {
        "g_mix": norm_mix[0][None, :],
        "w_in": _scale_query_columns(w_in[0]).astype(BF16),
        "w_fourier": w_fourier[0],
        "lb": lb_all[0],
        "g_o": norm_o[0][None, :],
        "w_out_a": w_out[0, :FOURIER_WIDTH].astype(BF16),
        "w_out_b": w_out[0, FOURIER_WIDTH:].astype(BF16),
        "g_ffn": norm_ffn[0][None, :],
        "w_router": router.astype(BF16),
    }
    xs_in = (x_prompt, x_sample)
    ps_in = (p_prompt[0], p_sample[0])

    counts = jnp.zeros((SUBLANES, LANES), F32)
    routed = []
    for x in xs_in:
        x1, slab, meta, counts = _mix_and_route(x, w, counts)
        routed.append((x1, slab, meta))

    total = sum(x.shape[0] * x.shape[1] for x in xs_in)
    n_tiles = total // EXPERT_TILE + N_CLASSES
    starts, tile_elo, tile_ehi, tile_rows, fill_tiles = _sorted_layout(counts, n_tiles, EXPERT_TILE)

    positions = [_positions(starts, meta) for _, _, meta in routed]
    xs = _dispatch(fill_tiles, positions, [r[1] for r in routed], n_tiles * EXPERT_TILE, tile=ROUTE_TILE)
    ys = _experts(tile_elo, tile_ehi, tile_rows, xs, w_exp_gate[0].astype(BF16), w_exp_up[0].astype(BF16),
                  w_exp_down[0].astype(BF16), tile=EXPERT_TILE)

    outs = []
    for (x1, _, _), pos, x, p in zip(routed, positions, xs_in, ps_in):
        out = _tail(pos, ys, x1, p.reshape(-1, PLE_DIM), norm_ple[0][None, :],
                    w_ple_gate[0].astype(BF16), w_ple_proj[0].astype(BF16), norm_final[None, :],
                    tile=ROUTE_TILE)
        outs.append(out.reshape(x.shape))
    return tuple(outs)
```
